```python
import jax, jax.numpy as jnp
from jax import lax
import numpy as np

D_MODEL = 1024
BATCH = 4
SEQ = 4096
DEPTH = 2

HEAD_DIM = 64
N_MIXERS = 4
GROUP_WIDTH = D_MODEL // N_MIXERS
GROUP_HEADS = GROUP_WIDTH // HEAD_DIM
D_MIX = N_MIXERS * GROUP_WIDTH
Q_BLOCK = 128
ROPE_THETA = 500000.0
ROPE_FRACTION = 4
EPS = 1e-6
NEG = -1e30

IDX_HEADS = 4
IDX_DIM = 32
DSA_TOPK = 256

CHUNK = 128

POOL_WINDOWS = (2, 4, 8, 16)
POOL_CH = GROUP_WIDTH // len(POOL_WINDOWS)

KV_DIM = HEAD_DIM
CMP_BLOCK = 32
CMP_STRIDE = 16
SEL_BLOCK = 64
SEL_TOPN = 16
WINDOW = 512

N_EXPERTS = 16
N_EXPERT_GROUPS = 4
EXPERTS_PER_GROUP = N_EXPERTS // N_EXPERT_GROUPS
TOP_K = 2
D_EXPERT = 256

PROJ_SPLITS = (GROUP_WIDTH, GROUP_WIDTH, GROUP_WIDTH, IDX_HEADS * IDX_DIM, IDX_DIM, IDX_HEADS,
               GROUP_WIDTH, GROUP_WIDTH,
               GROUP_WIDTH,
               GROUP_WIDTH, KV_DIM, KV_DIM, KV_DIM, KV_DIM, KV_DIM, KV_DIM, GROUP_HEADS * 3)
IN_WIDTH = sum(PROJ_SPLITS)

kernel_name = 'hybrid_dsa_gmlp_pool_nsa_moe'


def rmsnorm(x, g):
    xf = x.astype(jnp.float32)
    y = xf * lax.rsqrt(jnp.mean(xf * xf, axis=-1, keepdims=True) + EPS)
    return y.astype(x.dtype) * g


def partial_rope(x, pos):
    rd = x.shape[-1] // ROPE_FRACTION
    half = rd // 2
    inv_freq = ROPE_THETA ** (-jnp.arange(half, dtype=jnp.float32) / half)
    ang = pos.astype(jnp.float32)[:, :, None, None] * inv_freq
    cos, sin = jnp.cos(ang), jnp.sin(ang)
    xr = x[..., :rd].astype(jnp.float32)
    x1, x2 = xr[..., :half], xr[..., half:]
    rot = jnp.concatenate([x1 * cos - x2 * sin, x2 * cos + x1 * sin], axis=-1).astype(x.dtype)
    return jnp.concatenate([rot, x[..., rd:]], axis=-1)


def unblock(o):
    o = jnp.moveaxis(o, 0, 1)
    return o.reshape((o.shape[0], o.shape[1] * o.shape[2]) + o.shape[3:])


def gather_rows(table, idx):
    return jax.vmap(lambda t, i: t[i])(table, idx)


def split_projection(z):
    points = np.cumsum(np.array(PROJ_SPLITS))[:-1].tolist()
    return jnp.split(z, points, axis=-1)


def dsa_mixer(q, k, v, iq, ik, iw, pos):
    bsz, seq = q.shape[0], q.shape[1]
    topk = min(DSA_TOPK, seq // 4)
    q = partial_rope(q, pos)
    k = partial_rope(k, pos)
    iq = partial_rope(iq, pos).astype(jnp.float32)
    ik = partial_rope(ik[:, :, None, :], pos)[:, :, 0].astype(jnp.float32)
    iw = iw.astype(jnp.float32)
    key_pos = jnp.arange(seq)
    scale = HEAD_DIM ** -0.5

    def block(i):
        s0 = i * Q_BLOCK
        t_pos = s0 + jnp.arange(Q_BLOCK)
        qb = lax.dynamic_slice_in_dim(q, s0, Q_BLOCK, 1)
        iqb = lax.dynamic_slice_in_dim(iq, s0, Q_BLOCK, 1)
        iwb = lax.dynamic_slice_in_dim(iw, s0, Q_BLOCK, 1)
        idx_logits = jax.nn.relu(jnp.einsum('bthd,bsd->bths', iqb, ik))
        score = jnp.einsum('bth,bths->bts', iwb, idx_logits)
        causal = key_pos[None, :] <= t_pos[:, None]
        score = jnp.where(causal[None], score, -jnp.inf)
        _, sel = lax.top_k(score, topk)
        valid = sel <= t_pos[None, :, None]
        k_sel = gather_rows(k, sel)
        v_sel = gather_rows(v, sel)
        logits = jnp.einsum('bthd,btkhd->bhtk', qb, k_sel).astype(jnp.float32) * scale
        logits = jnp.where(valid[:, None], logits, -jnp.inf)
        p = jax.nn.softmax(logits, axis=-1).astype(v.dtype)
        return jnp.einsum('bhtk,btkhd->bthd', p, v_sel)

    out = unblock(lax.map(block, jnp.arange(seq // Q_BLOCK)))
    return out.reshape(bsz, seq, GROUP_WIDTH)


def gmlp_mixer(u, v, norm_g, w_s, b_s):
    bsz, seq = u.shape[0], u.shape[1]
    u = jax.nn.gelu(u)
    v = jax.nn.gelu(v)
    vf = v.astype(jnp.float32)
    mu = jnp.mean(vf, axis=-1, keepdims=True)
    var = jnp.mean(jnp.square(vf - mu), axis=-1, keepdims=True)
    v = ((vf - mu) * lax.rsqrt(var + EPS)).astype(u.dtype) * norm_g
    v = v.reshape(bsz, seq // CHUNK, CHUNK, GROUP_HEADS, HEAD_DIM)
    w_causal = jnp.where(jnp.tril(jnp.ones((CHUNK, CHUNK), dtype=bool)), w_s, 0)
    mixed = jnp.einsum('hts,bnshd->bnthd', w_causal, v) + jnp.swapaxes(b_s, 0, 1)[:, :, None]
    return u * mixed.reshape(bsz, seq, GROUP_WIDTH)


def pool_mixer(z, w_pool, scale):
    bsz, seq = z.shape[0], z.shape[1]
    zf = z.astype(jnp.float32).reshape(bsz, seq, len(POOL_WINDOWS), POOL_CH)
    csum = jnp.concatenate([jnp.zeros_like(zf[:, :1]), jnp.cumsum(zf, axis=1)], axis=1)
    t = jnp.arange(seq)
    pooled = []
    for g, w in enumerate(POOL_WINDOWS):
        lo = jnp.maximum(t + 1 - w, 0)
        count = (t + 1 - lo).astype(jnp.float32)[None, :, None]
        window_sum = csum[:, 1:, g] - csum[:, lo, g]
        pooled.append(window_sum / count - zf[:, :, g])
    pooled = jnp.stack(pooled, axis=2).astype(z.dtype)
    y = jnp.einsum('blgc,gce->blge', pooled, w_pool)
    return y.reshape(bsz, seq, GROUP_WIDTH) * scale


def nsa_mixer(q, kc, vc, ks, vs, kw, vw, gate_logits, w_cmp1, w_cmp2, cmp_pe, pos):
    bsz, seq = q.shape[0], q.shape[1]
    n_cmp = (seq - CMP_BLOCK) // CMP_STRIDE + 1
    n_sel = seq // SEL_BLOCK
    top_n = min(SEL_TOPN, n_sel)
    scale = HEAD_DIM ** -0.5
    t_all = jnp.arange(seq)
    q = partial_rope(q, pos)

    cmp_start = jnp.arange(n_cmp) * CMP_STRIDE
    cmp_idx = cmp_start[:, None] + jnp.arange(CMP_BLOCK)[None, :]

    def compress(kv, w1, w2, pe):
        blocks = kv[:, cmp_idx] + pe
        return jax.nn.gelu(jnp.einsum('bnjd,jde->bne', blocks, w1)) @ w2

    k_cmp = compress(kc, w_cmp1[0], w_cmp2[0], cmp_pe[0])
    v_cmp = compress(vc, w_cmp1[1], w_cmp2[1], cmp_pe[1])
    k_cmp = partial_rope(k_cmp[:, :, None], pos[:, cmp_start + CMP_BLOCK // 2])[:, :, 0]
    cmp_visible = (cmp_start + CMP_BLOCK - 1)[None, :] <= t_all[:, None]
    logits = jnp.einsum('blhd,bnd->bhln', q, k_cmp).astype(jnp.float32) * scale
    p_cmp = jax.nn.softmax(jnp.where(cmp_visible, logits, NEG), axis=-1)
    p_cmp = jnp.where(cmp_visible.any(-1)[:, None], p_cmp, 0.0)
    o_cmp = jnp.einsum('bhln,bnd->blhd', p_cmp.astype(vc.dtype), v_cmp)

    sel_start = jnp.arange(n_sel) * SEL_BLOCK
    overlap = jnp.clip(jnp.minimum(sel_start[:, None] + SEL_BLOCK, cmp_start[None, :] + CMP_BLOCK)
                       - jnp.maximum(sel_start[:, None], cmp_start[None, :]), 0)
    overlap = overlap.astype(jnp.float32) / CMP_BLOCK
    importance = jnp.einsum('bhln,jn->blj', p_cmp, overlap)
    blk = jnp.arange(n_sel)
    admissible = sel_start[None, :] <= t_all[:, None]
    forced = (blk[None, :] == (t_all // SEL_BLOCK)[:, None]) | (blk[None, :] == 0)
    importance = jnp.where(forced, jnp.inf, importance)
    importance = jnp.where(admissible, importance, -jnp.inf)
    _, sel_idx = lax.top_k(importance, top_n)
    sel_valid = sel_idx * SEL_BLOCK <= t_all[None, :, None]

    ks = partial_rope(ks[:, :, None], pos)[:, :, 0].reshape(bsz, n_sel, SEL_BLOCK, KV_DIM)
    vs = vs.reshape(bsz, n_sel, SEL_BLOCK, KV_DIM)
    kw = partial_rope(kw[:, :, None], pos)[:, :, 0]
    kw_pad = jnp.pad(kw, ((0, 0), (WINDOW, 0), (0, 0)))
    vw_pad = jnp.pad(vw, ((0, 0), (WINDOW, 0), (0, 0)))
    in_block = jnp.arange(SEL_BLOCK)
    win_off = jnp.arange(WINDOW + Q_BLOCK) - WINDOW

    def block(i):
        s0 = i * Q_BLOCK
        t_pos = s0 + jnp.arange(Q_BLOCK)
        qb = lax.dynamic_slice_in_dim(q, s0, Q_BLOCK, 1)
        idx = lax.dynamic_slice_in_dim(sel_idx, s0, Q_BLOCK, 1)
        ok = lax.dynamic_slice_in_dim(sel_valid, s0, Q_BLOCK, 1)
        k_g = gather_rows(ks, idx).reshape(bsz, Q_BLOCK, top_n * SEL_BLOCK, KV_DIM)
        v_g = gather_rows(vs, idx).reshape(bsz, Q_BLOCK, top_n * SEL_BLOCK, KV_DIM)
        key_pos = idx[..., None] * SEL_BLOCK + in_block
        mask = (ok[..., None] & (key_pos <= t_pos[None, :, None, None])).reshape(bsz, Q_BLOCK, -1)
        logits_s = jnp.einsum('bthd,btkd->bhtk', qb, k_g).astype(jnp.float32) * scale
        logits_s = jnp.where(mask[:, None], logits_s, -jnp.inf)
        o_sel = jnp.einsum('bhtk,btkd->bthd', jax.nn.softmax(logits_s, axis=-1).astype(vs.dtype), v_g)
        k_win = lax.dynamic_slice_in_dim(kw_pad, s0, WINDOW + Q_BLOCK, 1)
        v_win = lax.dynamic_slice_in_dim(vw_pad, s0, WINDOW + Q_BLOCK, 1)
        kpos = s0 + win_off
        wmask = ((kpos[None, :] <= t_pos[:, None]) & (kpos[None, :] > t_pos[:, None] - WINDOW)
                 & (kpos[None, :] >= 0))
        logits_w = jnp.einsum('bthd,bsd->bhts', qb, k_win).astype(jnp.float32) * scale
        logits_w = jnp.where(wmask, logits_w, -jnp.inf)
        o_win = jnp.einsum('bhts,bsd->bthd', jax.nn.softmax(logits_w, axis=-1).astype(vw.dtype), v_win)
        return o_sel, o_win

    o_sel, o_win = lax.map(block, jnp.arange(seq // Q_BLOCK))
    o_sel = unblock(o_sel)
    o_win = unblock(o_win)
    g = jax.nn.sigmoid(gate_logits.astype(jnp.float32)).astype(q.dtype)
    out = g[..., 0:1] * o_cmp + g[..., 1:2] * o_sel + g[..., 2:3] * o_win
    return out.reshape(bsz, seq, GROUP_WIDTH)


def moe(h, router_w, router_b, w_gate, w_up, w_down):
    bsz, seq, d = h.shape
    tok = h.reshape(-1, d)
    affinity = jax.nn.sigmoid((tok @ router_w).astype(jnp.float32))
    biased = affinity + router_b.astype(jnp.float32)
    group_score = lax.top_k(biased.reshape(-1, N_EXPERT_GROUPS, EXPERTS_PER_GROUP), TOP_K)[0].sum(-1)
    best_group = jnp.argmax(group_score, axis=-1)
    in_group = (jnp.arange(N_EXPERTS) // EXPERTS_PER_GROUP)[None, :] == best_group[:, None]
    _, top_idx = lax.top_k(jnp.where(in_group, biased, -jnp.inf), TOP_K)
    top_aff = jnp.take_along_axis(affinity, top_idx, axis=-1)
    weights = top_aff / jnp.sum(top_aff, axis=-1, keepdims=True)
    gate = jnp.sum(jax.nn.one_hot(top_idx, N_EXPERTS, dtype=jnp.float32) * weights[..., None], axis=1)
    gate = gate.astype(h.dtype)
    out = jnp.zeros_like(tok)
    for e in range(N_EXPERTS):
        hidden = jax.nn.silu(tok @ w_gate[e]) * (tok @ w_up[e])
        out = out + gate[:, e:e + 1] * (hidden @ w_down[e])
    return out.reshape(bsz, seq, d)


def hybrid_layer(x, c_act, pos, ada_w, ada_b, g_mix, g_ffn, w_in, sgu_g, w_sgu, b_sgu, w_pool, pool_scale,
                 w_cmp1, w_cmp2, cmp_pe, w_out, router_w, router_b, w_gate, w_up, w_down):
    bsz, seq = x.shape[0], x.shape[1]
    mod = c_act @ ada_w + ada_b
    shift_m, scale_m, gate_m, shift_f, scale_f, gate_f = [m[:, None, :] for m in jnp.split(mod, 6, axis=-1)]

    h = rmsnorm(x, g_mix) * (1 + scale_m) + shift_m
    (a_q, a_k, a_v, a_iq, a_ik, a_iw, b_u, b_v, c_z,
     d_q, d_kc, d_vc, d_ks, d_vs, d_kw, d_vw, d_g) = split_projection(h @ w_in)
    o_a = dsa_mixer(a_q.reshape(bsz, seq, GROUP_HEADS, HEAD_DIM), a_k.reshape(bsz, seq, GROUP_HEADS, HEAD_DIM),
                    a_v.reshape(bsz, seq, GROUP_HEADS, HEAD_DIM), a_iq.reshape(bsz, seq, IDX_HEADS, IDX_DIM),
                    a_ik, a_iw, pos)
    o_b = gmlp_mixer(b_u, b_v, sgu_g, w_sgu, b_sgu)
    o_c = pool_mixer(c_z, w_pool, pool_scale)
    o_d = nsa_mixer(d_q.reshape(bsz, seq, GROUP_HEADS, HEAD_DIM), d_kc, d_vc, d_ks, d_vs, d_kw, d_vw,
                    d_g.reshape(bsz, seq, GROUP_HEADS, 3), w_cmp1, w_cmp2, cmp_pe, pos)
    mixed = jnp.concatenate([o_a, o_b, o_c, o_d], axis=-1) @ w_out
    x = x + gate_m * mixed

    h = rmsnorm(x, g_ffn) * (1 + scale_f) + shift_f
    return x + gate_f * moe(h, router_w, router_b, w_gate, w_up, w_down)


def setup_inputs(seed: int = 0) -> dict:
    key = jax.random.key(seed)
    ks = jax.random.split(key, 24)

    def nrm(k, shape, s):
        return jax.random.normal(k, shape, jnp.float32) * s

    def gain(k, shape):
        return 1.0 + nrm(k, shape, 0.02)

    return {
        'x': nrm(ks[0], (BATCH, SEQ, D_MODEL), 1.0),
        'c': nrm(ks[1], (BATCH, D_MODEL), 1.0),
        'positions': jnp.arange(SEQ, dtype=jnp.int32)[None, :]
                     + jax.random.randint(ks[2], (BATCH, 1), 0, 1024, dtype=jnp.int32),
        'ada_w': nrm(ks[3], (DEPTH, D_MODEL, 6 * D_MODEL), 0.5 * D_MODEL ** -0.5),
        'ada_b': nrm(ks[4], (DEPTH, 6 * D_MODEL), 0.02),
        'norm_mix_g': gain(ks[5], (DEPTH, D_MODEL)),
        'norm_ffn_g': gain(ks[6], (DEPTH, D_MODEL)),
        'w_in': nrm(ks[7], (DEPTH, D_MODEL, IN_WIDTH), D_MODEL ** -0.5),
        'sgu_norm_g': gain(ks[8], (DEPTH, GROUP_WIDTH)),
        'w_sgu': nrm(ks[9], (DEPTH, GROUP_HEADS, CHUNK, CHUNK), CHUNK ** -0.5),
        'b_sgu': gain(ks[10], (DEPTH, GROUP_HEADS, CHUNK)),
        'w_pool': nrm(ks[11], (DEPTH, len(POOL_WINDOWS), POOL_CH, POOL_CH), POOL_CH ** -0.5),
        'pool_scale': 1.0 + nrm(ks[12], (DEPTH, GROUP_WIDTH), 0.1),
        'w_cmp1': nrm(ks[13], (DEPTH, 2, CMP_BLOCK, KV_DIM, KV_DIM), (CMP_BLOCK * KV_DIM) ** -0.5),
        'w_cmp2': nrm(ks[14], (DEPTH, 2, KV_DIM, KV_DIM), KV_DIM ** -0.5),
        'cmp_pe': nrm(ks[15], (DEPTH, 2, CMP_BLOCK, KV_DIM), 0.1),
        'w_out': nrm(ks[16], (DEPTH, D_MIX, D_MODEL), D_MIX ** -0.5),
        'router_w': nrm(ks[17], (D_MODEL, N_EXPERTS), D_MODEL ** -0.5),
        'router_b': nrm(ks[18], (N_EXPERTS,), 0.01),
        'w_gate': nrm(ks[19], (DEPTH, N_EXPERTS, D_MODEL, D_EXPERT), D_MODEL ** -0.5),
        'w_up': nrm(ks[20], (DEPTH, N_EXPERTS, D_MODEL, D_EXPERT), D_MODEL ** -0.5),
        'w_down': nrm(ks[21], (DEPTH, N_EXPERTS, D_EXPERT, D_MODEL), D_EXPERT ** -0.5),
        'final_norm_g': gain(ks[22], (D_MODEL,)),
    }


def reference(x, c, positions, ada_w, ada_b, norm_mix_g, norm_ffn_g, w_in, sgu_norm_g, w_sgu, b_sgu, w_pool,
              pool_scale, w_cmp1, w_cmp2, cmp_pe, w_out, router_w, router_b, w_gate, w_up, w_down, final_norm_g):
    c_act = jax.nn.silu(c)
    for layer in range(DEPTH):
        x = hybrid_layer(x, c_act, positions, ada_w[layer], ada_b[layer], norm_mix_g[layer], norm_ffn_g[layer],
                         w_in[layer], sgu_norm_g[layer], w_sgu[layer], b_sgu[layer], w_pool[layer],
                         pool_scale[layer], w_cmp1[layer], w_cmp2[layer], cmp_pe[layer], w_out[layer],
                         router_w, router_b, w_gate[layer], w_up[layer], w_down[layer])
    return rmsnorm(x, final_norm_g)
```

```python
import functools
import math

import numpy as np
import jax
import jax.numpy as jnp
from jax import lax
from jax.experimental import pallas as pl
from jax.experimental.pallas import tpu as pltpu

F32 = jnp.float32
BF16 = jnp.bfloat16
I32 = jnp.int32
HIGHEST = lax.Precision.HIGHEST

LANES = 128
HEAD_DIM = 64
GROUP_WIDTH = 256
GROUP_HEADS = 4
ROPE_THETA = 500000.0
EPS = 1e-6
NEG = -1e30
IDX_DIM = 32
DSA_TOPK = 256
CMP_BLOCK = 32
CMP_STRIDE = 16
SEL_BLOCK = 64
SEL_TOPN = 16
WINDOW = 512
N_EXPERTS = 16
EXPERTS_PER_GROUP = 4
N_EXPERT_GROUPS = 4
D_EXPERT = 256
POOL_WINDOWS = (2, 4, 8, 16)

Q_TILE = 128
KEY_CHUNK = 512
TOKEN_TILE = 512
VMEM_LIMIT = 56 * 1024 * 1024

INT_MIN = -2 ** 31

_OFF = {}
_o = 0
for _name, _w in (("a_q", 256), ("a_k", 256), ("a_v", 256), ("a_iq", 128), ("a_ik", 32), ("a_iw", 4),
                  ("b_u", 256), ("b_v", 256), ("c_z", 256), ("d_q", 256), ("d_kc", 64), ("d_vc", 64),
                  ("d_ks", 64), ("d_vs", 64), ("d_kw", 64), ("d_vw", 64), ("d_g", 12)):
    _OFF[_name] = (_o, _o + _w)
    _o += _w
ROW_WIDTH = 2048
COL_ROWS = 400


def _cparams(sem):
    return pltpu.CompilerParams(dimension_semantics=sem, vmem_limit_bytes=VMEM_LIMIT)


def _sigmoid(x):
    return 1.0 / (1.0 + jnp.exp(-x))


def _gelu(x):
    return 0.5 * x * (1.0 + jnp.tanh(0.7978845608028654 * (x + 0.044715 * (x * x * x))))


def _dot(a, b, **kw):
    return jnp.dot(a, b, preferred_element_type=F32, **kw)


def _dot_nt(a, b, **kw):
    return lax.dot_general(a, b, (((1,), (1,)), ((), ())), preferred_element_type=F32, **kw)


def _mod_kernel(c_ref, w_ref, b_ref, o_ref):
    c = c_ref[...]
    ca = c * _sigmoid(c)
    o_ref[0] = _dot(ca, w_ref[0], precision=HIGHEST) + b_ref[0]


def _adaln_mod(c, ada_w, ada_b):
    depth, d, d6 = ada_w.shape
    bsz = c.shape[0]
    rows = 8
    c_pad = jnp.zeros((rows, d), F32).at[:bsz].set(c)
    tn = 1536
    out = pl.pallas_call(
        _mod_kernel,
        grid=(depth, d6 // tn),
        in_specs=[pl.BlockSpec((rows, d), lambda l, j: (0, 0)),
                  pl.BlockSpec((1, d, tn), lambda l, j: (l, 0, j)),
                  pl.BlockSpec((1, 1, tn), lambda l, j: (l, 0, j))],
        out_specs=pl.BlockSpec((1, rows, tn), lambda l, j: (l, 0, j)),
        out_shape=jax.ShapeDtypeStruct((depth, rows, d6), F32),
        compiler_params=_cparams(("arbitrary", "arbitrary")),
        name="adaln_mod",
    )(c_pad, ada_w, ada_b.reshape(depth, 1, d6))
    mod = out[:, :bsz].reshape(depth, bsz, 6, d)
    return jnp.concatenate([mod, jnp.zeros((depth, bsz, 2, d), F32)], axis=2)


def _freq_row(hd):
    rd = hd // 4
    half = rd // 2
    inv = (ROPE_THETA ** (-np.arange(half, dtype=np.float32) / np.float32(half))).astype(np.float32)
    row = np.zeros((1, LANES), np.float32)
    for lane in range(LANES):
        r = lane % hd
        if r < rd:
            row[0, lane] = inv[r % half]
    return jnp.asarray(row)


def _rope_tables(pos_col, frow, hd):
    half = hd // 8
    ang = pos_col * frow
    c = jnp.cos(ang)
    s = jnp.sin(ang)
    r = lax.broadcasted_iota(I32, (1, LANES), 1) % hd
    sa = jnp.where(r < half, -s, 0.0)
    sb = jnp.where((r >= half) & (r < 2 * half), s, 0.0)
    return c, sa, sb


def _rope_kernel(pos_ref, f64_ref, f32_ref, c64, sa64, sb64, c32, sa32, sb32):
    pos = pos_ref[0].astype(F32)
    c, sa, sb = _rope_tables(pos, f64_ref[...], HEAD_DIM)
    c64[0] = c
    sa64[0] = sa
    sb64[0] = sb
    c, sa, sb = _rope_tables(pos, f32_ref[...], IDX_DIM)
    c32[0] = c
    sa32[0] = sa
    sb32[0] = sb


def _rope_call(positions):
    bsz, n = positions.shape
    tm = TOKEN_TILE
    tab = jax.ShapeDtypeStruct((bsz, n, LANES), F32)
    tspec = pl.BlockSpec((1, tm, LANES), lambda b, i: (b, i, 0))
    fspec = pl.BlockSpec((1, LANES), lambda b, i: (0, 0))
    return pl.pallas_call(
        _rope_kernel,
        grid=(bsz, n // tm),
        in_specs=[pl.BlockSpec((1, tm, 1), lambda b, i: (b, i, 0)), fspec, fspec],
        out_specs=[tspec] * 6,
        out_shape=[tab] * 6,
        compiler_params=_cparams(("parallel", "arbitrary")),
        name="rope_tables",
    )(positions.reshape(bsz, n, 1), _freq_row(HEAD_DIM), _freq_row(IDX_DIM))


def _apply_rope(x, c, sa, sb, half):
    return x * c + pltpu.roll(x, LANES - half, 1) * sa + pltpu.roll(x, half, 1) * sb


def _inproj_kernel(x_ref, mod_ref, g_ref, w_ref, wt_ref, c64, sa64, sb64, c32, sa32, sb32,
                   aq_ref, ak_ref, dq_ref, dk_ref, iq_ref, ik_ref, u_ref, v_ref, cz_ref, dc_ref,
                   vt_ref, st_ref):
    x = x_ref[0]
    ms = jnp.mean(x * x, axis=-1, keepdims=True)
    y = x * lax.rsqrt(ms + EPS) * g_ref[...]
    h = y * (1.0 + mod_ref[0, 1:2, :]) + mod_ref[0, 0:1, :]
    hb = h.astype(BF16)
    z = _dot(hb, w_ref[...])
    zt = _dot_nt(wt_ref[...], hb)

    c, sa, sb = c64[0], sa64[0], sb64[0]
    tiles = [_apply_rope(z[:, LANES * t:LANES * (t + 1)], c, sa, sb, 8) for t in range(7)]
    scale = HEAD_DIM ** -0.5
    aq_ref[0] = jnp.concatenate([tiles[0] * scale, tiles[1] * scale], axis=1).astype(BF16)
    ak_ref[0] = jnp.concatenate([tiles[2], tiles[3]], axis=1).astype(BF16)
    dq_ref[0] = jnp.concatenate([tiles[4] * scale, tiles[5] * scale], axis=1)
    dk_ref[0] = tiles[6].astype(BF16)
    c, sa, sb = c32[0], sa32[0], sb32[0]
    iq_ref[0] = _apply_rope(z[:, 896:1024], c, sa, sb, 4)
    ik_ref[0] = _apply_rope(z[:, 1024:1152], c, sa, sb, 4)
    u_ref[0] = z[:, 1152:1408]
    v_ref[0] = z[:, 1408:1664]
    cz_ref[0] = z[:, 1664:1920]
    dc_ref[0] = z[:, 1920:2048]
    vt_ref[0] = zt[0:384].astype(BF16)
    st_ref[0] = zt[384:400]


def _pack_w_in(w_in):
    def col(name):
        a, b = _OFF[name]
        return w_in[:, a:b]
    w_row = jnp.concatenate([col("a_q"), col("a_k"), col("d_q"), col("d_ks"), col("d_kw"), col("a_iq"),
                             col("a_ik"), col("a_ik"), col("a_ik"), col("a_ik"),
                             col("b_u"), col("b_v"), col("c_z"), col("d_kc"), col("d_vc")], axis=1)
    w_col = jnp.concatenate([col("a_v"), col("d_vs"), col("d_vw"), col("a_iw"), col("d_g")], axis=1).T
    return w_row.astype(BF16), w_col.astype(BF16)


def _inproj_call(x, mod, g, w_in, tables):
    bsz, n, d = x.shape
    tm = TOKEN_TILE
    w_row, w_col = _pack_w_in(w_in)
    tok = lambda width: pl.BlockSpec((1, tm, width), lambda b, i: (b, i, 0))
    shp = lambda width, dt: jax.ShapeDtypeStruct((bsz, n, width), dt)
    const2 = lambda shape: pl.BlockSpec(shape, lambda b, i: (0, 0))
    return pl.pallas_call(
        _inproj_kernel,
        grid=(bsz, n // tm),
        in_specs=[tok(d), pl.BlockSpec((1, 8, d), lambda b, i: (b, 0, 0)), const2((1, d)),
                  const2((d, ROW_WIDTH)), const2((COL_ROWS, d))] + [tok(LANES)] * 6,
        out_specs=[tok(256), tok(256), tok(256), tok(128), tok(128), tok(128), tok(256), tok(256), tok(256),
                   tok(128),
                   pl.BlockSpec((1, 384, tm), lambda b, i: (b, 0, i)),
                   pl.BlockSpec((1, 16, tm), lambda b, i: (b, 0, i))],
        out_shape=[shp(256, BF16), shp(256, BF16), shp(256, F32), shp(128, BF16), shp(128, F32), shp(128, F32),
                   shp(256, F32), shp(256, F32), shp(256, F32), shp(128, F32),
                   jax.ShapeDtypeStruct((bsz, 384, n), BF16),
                   jax.ShapeDtypeStruct((bsz, 16, n), F32)],
        compiler_params=_cparams(("parallel", "arbitrary")),
        name="inproj",
    )(x, mod, g.reshape(1, d), w_row, w_col, *tables)


def _cmp_kernel(kf_ref, vf_ref, w1a_ref, w1b_ref, w1_ref, pe_ref, w2k_ref, w2vt_ref, pos_ref, f_ref,
                kc_ref, vct_ref):
    nc = kf_ref.shape[1]

    def hidden(flat, idx):
        fb = flat.astype(BF16)
        a = _dot(fb, w1a_ref[idx])
        b = _dot(fb, w1b_ref[idx])
        b_next = pltpu.roll(b, nc - 1, 0)
        bias = _dot(jnp.broadcast_to(pe_ref[idx], (8, pe_ref.shape[2])), w1_ref[idx], precision=HIGHEST)[0:1]
        return _gelu(a + b_next + bias).astype(BF16)

    k_c = _dot(hidden(kf_ref[0], 0), w2k_ref[...])
    c, sa, sb = _rope_tables(pos_ref[0].astype(F32), f_ref[...], HEAD_DIM)
    kc_ref[0] = _apply_rope(k_c, c, sa, sb, 8).astype(BF16)
    vct_ref[0] = _dot_nt(w2vt_ref[...], hidden(vf_ref[0], 1)).astype(BF16)


def _cmp_call(dc, w_cmp1, w_cmp2, cmp_pe, positions):
    bsz, n, _ = dc.shape
    nc = n // CMP_STRIDE
    half = CMP_STRIDE * HEAD_DIM
    flat = dc.reshape(bsz, nc, CMP_STRIDE, 2, HEAD_DIM).transpose(3, 0, 1, 2, 4).reshape(2, bsz, nc, half)
    w1 = w_cmp1.reshape(2, CMP_BLOCK * HEAD_DIM, HEAD_DIM)
    w1a = w1[:, :half].astype(BF16)
    w1b = w1[:, half:].astype(BF16)
    pe = cmp_pe.reshape(2, 1, CMP_BLOCK * HEAD_DIM)
    w2k = jnp.concatenate([w_cmp2[0], jnp.zeros_like(w_cmp2[0])], axis=1).astype(BF16)
    w2vt = w_cmp2[1].T.astype(BF16)
    mid = jnp.minimum(jnp.arange(nc) * CMP_STRIDE + CMP_BLOCK // 2, n - 1)
    pos_c = positions[:, mid].reshape(bsz, nc, 1)
    per_b = lambda shape: pl.BlockSpec(shape, lambda b: (b, 0, 0))
    const = lambda shape: pl.BlockSpec(shape, lambda b: tuple(0 for _ in shape))
    return pl.pallas_call(
        _cmp_kernel,
        grid=(bsz,),
        in_specs=[per_b((1, nc, half)), per_b((1, nc, half)), const((2, half, HEAD_DIM)), const((2, half, HEAD_DIM)),
                  const((2, 2 * half, HEAD_DIM)), const((2, 1, 2 * half)), const((HEAD_DIM, LANES)),
                  const((HEAD_DIM, HEAD_DIM)), per_b((1, nc, 1)), const((1, LANES))],
        out_specs=[per_b((1, nc, LANES)), per_b((1, HEAD_DIM, nc))],
        out_shape=[jax.ShapeDtypeStruct((bsz, nc, LANES), BF16), jax.ShapeDtypeStruct((bsz, HEAD_DIM, nc), BF16)],
        compiler_params=_cparams(("arbitrary",)),
        name="nsa_compress",
    )(flat[0], flat[1], w1a, w1b, w1, pe, w2k, w2vt, pos_c, _freq_row(HEAD_DIM))


def _head_masked(q, h, width):
    lane = lax.broadcasted_iota(I32, (1, width), 1)
    return jnp.where((lane // HEAD_DIM) == h, q, jnp.zeros_like(q))


def _online_update(state, s, sel, vt):
    m, l, acc = state
    s = jnp.where(sel, s, NEG)
    m_new = jnp.maximum(m, jnp.max(s, axis=0, keepdims=True))
    alpha = jnp.exp(m - m_new)
    p = jnp.where(sel, jnp.exp(s - m_new), 0.0)
    l = alpha * l + jnp.sum(p, axis=0, keepdims=True)
    acc = alpha * acc + _dot(vt, p.astype(BF16))
    return m_new, l, acc


def _init_state():
    return (jnp.full((1, Q_TILE), NEG, F32), jnp.zeros((1, Q_TILE), F32), jnp.zeros((HEAD_DIM, Q_TILE), F32))


def _dsa_kernel(q_ref, k_ref, vt_ref, iq_ref, ik_ref, st_ref, tri_ref, o_ref, keys_ref):
    i = pl.program_id(1)
    nck = (i * Q_TILE + Q_TILE + KEY_CHUNK - 1) // KEY_CHUNK
    ck = KEY_CHUNK
    t_row = i * Q_TILE + lax.broadcasted_iota(I32, (1, Q_TILE), 1)
    s_col = lax.broadcasted_iota(I32, (ck, 1), 0)

    iq = iq_ref[0]
    iq_h = [jnp.where((lax.broadcasted_iota(I32, (1, LANES), 1) // IDX_DIM) == h, iq, 0.0) for h in range(4)]
    iw = [st_ref[0, h:h + 1, :] for h in range(4)]

    def score_body(c, _):
        base = pl.multiple_of(c * ck, ck)
        ikc = ik_ref[0, pl.ds(base, ck), :]
        sc = jnp.zeros((ck, Q_TILE), F32)
        for h in range(4):
            sc = sc + iw[h] * jnp.maximum(_dot_nt(ikc, iq_h[h], precision=HIGHEST), 0.0)
        sc = jnp.where(base + s_col <= t_row, sc + 0.0, -jnp.inf)
        bits = pltpu.bitcast(sc, I32)
        keys_ref[pl.ds(base, ck), :] = bits ^ ((bits >> 31) & jnp.int32(0x7FFFFFFF))
        return 0

    lax.fori_loop(0, nck, score_body, 0)

    def count(pred):
        def body(c, acc):
            kc = keys_ref[pl.ds(pl.multiple_of(c * ck, ck), ck), :]
            return acc + jnp.sum(jnp.where(pred(kc), 1, 0).astype(I32).reshape(ck // 8, 8, Q_TILE), axis=0)
        acc = lax.fori_loop(0, nck, body, jnp.zeros((8, Q_TILE), I32))
        return jnp.sum(acc, axis=0, keepdims=True)

    def bit_body(bi, thr):
        bit = jnp.full((1, Q_TILE), 1, I32) << (31 - bi)
        cand = thr ^ bit
        return jnp.where(count(lambda kc: kc >= cand) >= DSA_TOPK, cand, thr)

    thr = lax.fori_loop(0, 32, bit_body, jnp.full((1, Q_TILE), INT_MIN, I32))
    need = (DSA_TOPK - count(lambda kc: kc > thr)).astype(F32)

    q = q_ref[0]
    q_h = [_head_masked(q, h, GROUP_WIDTH) for h in range(4)]

    def attn_body(c, carry):
        states, tie_carry = carry
        base = pl.multiple_of(c * ck, ck)
        kc = keys_ref[pl.ds(base, ck), :]
        eq = kc == thr
        eq_f = jnp.where(eq, 1.0, 0.0)
        rank = _dot(tri_ref[...], eq_f.astype(BF16)) + tie_carry
        sel = ((kc > thr) | (eq & (rank < need))) & (base + s_col <= t_row)
        kk = k_ref[0, pl.ds(base, ck), :]
        new_states = []
        for h in range(4):
            s = _dot_nt(kk, q_h[h])
            vt = vt_ref[0, h * HEAD_DIM:(h + 1) * HEAD_DIM, pl.ds(base, ck)]
            new_states.append(_online_update(states[h], s, sel, vt))
        return tuple(new_states), tie_carry + jnp.sum(eq_f, axis=0, keepdims=True)

    states, _ = lax.fori_loop(0, nck, attn_body,
                              (tuple(_init_state() for _ in range(4)), jnp.zeros((1, Q_TILE), F32)))
    out_t = jnp.concatenate([acc / l for (_, l, acc) in states], axis=0)
    o_ref[0] = out_t.T.astype(BF16)


def _dsa_call(aq, ak, vt_all, iq, ik4, st):
    bsz, n, _ = aq.shape
    tri = jnp.asarray(np.tril(np.ones((KEY_CHUNK, KEY_CHUNK), np.float32), -1)).astype(BF16)
    return pl.pallas_call(
        _dsa_kernel,
        grid=(bsz, n // Q_TILE),
        in_specs=[pl.BlockSpec((1, Q_TILE, GROUP_WIDTH), lambda b, i: (b, i, 0)),
                  pl.BlockSpec((1, n, GROUP_WIDTH), lambda b, i: (b, 0, 0)),
                  pl.BlockSpec((1, GROUP_WIDTH, n), lambda b, i: (b, 0, 0)),
                  pl.BlockSpec((1, Q_TILE, LANES), lambda b, i: (b, i, 0)),
                  pl.BlockSpec((1, n, LANES), lambda b, i: (b, 0, 0)),
                  pl.BlockSpec((1, 16, Q_TILE), lambda b, i: (b, 0, i)),
                  pl.BlockSpec((KEY_CHUNK, KEY_CHUNK), lambda b, i: (0, 0))],
        out_specs=pl.BlockSpec((1, Q_TILE, GROUP_WIDTH), lambda b, i: (b, i, 0)),
        out_shape=jax.ShapeDtypeStruct((bsz, n, GROUP_WIDTH), BF16),
        scratch_shapes=[pltpu.VMEM((n, Q_TILE), I32)],
        compiler_params=_cparams(("parallel", "arbitrary")),
        name="dsa_attention",
    )(aq, ak, vt_all, iq, ik4, st, tri)


def _nsa_kernel(q_ref, kc_ref, vct_ref, dk_ref, vst_ref, vwt_ref, st_ref, ov_ref, o_ref, sel_ref):
    i = pl.program_id(1)
    n = dk_ref.shape[1]
    nc = kc_ref.shape[1]
    nsel = ov_ref.shape[0]
    ck = KEY_CHUNK
    nck = (i * Q_TILE + Q_TILE + ck - 1) // ck
    t_row = i * Q_TILE + lax.broadcasted_iota(I32, (1, Q_TILE), 1)
    lane = lax.broadcasted_iota(I32, (1, LANES), 1)

    q = q_ref[0]
    q_sel, q_win = [], []
    for h in range(4):
        tile = q[:, LANES * (h // 2):LANES * (h // 2 + 1)]
        swapped = pltpu.roll(tile, HEAD_DIM, 1)
        low, high = (tile, swapped) if h % 2 == 0 else (swapped, tile)
        q_sel.append(jnp.where(lane < HEAD_DIM, low, 0.0).astype(BF16))
        q_win.append(jnp.where(lane >= HEAD_DIM, high, 0.0).astype(BF16))

    n_col = lax.broadcasted_iota(I32, (nc, 1), 0)
    vis = n_col * CMP_STRIDE + (CMP_BLOCK - 1) <= t_row
    kcmp = kc_ref[0]
    o_cmp = []
    p_sum = jnp.zeros((nc, Q_TILE), F32)
    for h in range(4):
        s = jnp.where(vis, _dot_nt(kcmp, q_sel[h]), NEG)
        m = jnp.max(s, axis=0, keepdims=True)
        p = jnp.where(vis, jnp.exp(s - m), 0.0)
        l = jnp.sum(p, axis=0, keepdims=True)
        p = p * jnp.where(l > 0.0, 1.0 / l, 0.0)
        p_sum = p_sum + p
        o_cmp.append(_dot(vct_ref[0], p.astype(BF16)))

    imp = _dot(ov_ref[...], p_sum, precision=HIGHEST)
    j_col = lax.broadcasted_iota(I32, (nsel, 1), 0)
    forced = (j_col == t_row // SEL_BLOCK) | (j_col == 0)
    admissible = j_col * SEL_BLOCK <= t_row
    imp = jnp.where(forced, jnp.inf, imp)
    imp = jnp.where(admissible, imp, -jnp.inf)
    rank = jnp.zeros((nsel, Q_TILE), F32)
    for jp in range(nsel):
        row = imp[jp:jp + 1, :]
        ahead = (row > imp) | ((row == imp) & (jp < j_col))
        rank = rank + jnp.where(ahead, 1.0, 0.0)
    sel_ref[...] = jnp.where((rank < float(min(SEL_TOPN, nsel))) & admissible, 1.0, 0.0)

    s_col = lax.broadcasted_iota(I32, (ck, 1), 0)
    per_chunk = ck // SEL_BLOCK

    def sel_body(c, states):
        base = pl.multiple_of(c * ck, ck)
        rows = sel_ref[pl.ds(pl.multiple_of(c * per_chunk, per_chunk), per_chunk), :]
        blk = jnp.concatenate([jnp.broadcast_to(rows[r:r + 1, :], (SEL_BLOCK, Q_TILE)) for r in range(per_chunk)],
                              axis=0)
        sel = (blk > 0.5) & (base + s_col <= t_row)
        kk = dk_ref[0, pl.ds(base, ck), :]
        vt = vst_ref[0, :, pl.ds(base, ck)]
        return tuple(_online_update(states[h], _dot_nt(kk, q_sel[h]), sel, vt) for h in range(4))

    st_sel = lax.fori_loop(0, nck, sel_body, tuple(_init_state() for _ in range(4)))

    span = WINDOW + Q_TILE
    start = pl.multiple_of(jnp.maximum(i * Q_TILE - WINDOW, 0), Q_TILE)
    kpos = start + lax.broadcasted_iota(I32, (span, 1), 0)
    wmask = (kpos <= t_row) & (kpos > t_row - WINDOW)
    kw = dk_ref[0, pl.ds(start, span), :]
    vw = vwt_ref[0, :, pl.ds(start, span)]
    st_win = [_online_update(_init_state(), _dot_nt(kw, q_win[h]), wmask, vw) for h in range(4)]

    outs = []
    for h in range(4):
        g = [_sigmoid(st_ref[0, 4 + 3 * h + r:5 + 3 * h + r, :]) for r in range(3)]
        o_s = st_sel[h][2] / st_sel[h][1]
        o_w = st_win[h][2] / st_win[h][1]
        outs.append(g[0] * o_cmp[h] + g[1] * o_s + g[2] * o_w)
    o_ref[0] = jnp.concatenate(outs, axis=0).T.astype(BF16)


def _overlap_matrix(n):
    nc = n // CMP_STRIDE
    nsel = n // SEL_BLOCK
    sel_start = np.arange(nsel) * SEL_BLOCK
    cmp_start = np.arange(nc) * CMP_STRIDE
    ov = np.clip(np.minimum(sel_start[:, None] + SEL_BLOCK, cmp_start[None, :] + CMP_BLOCK)
                 - np.maximum(sel_start[:, None], cmp_start[None, :]), 0, None).astype(np.float32) / CMP_BLOCK
    ov[:, nc - 1] = 0.0
    return jnp.asarray(ov)


def _nsa_call(dq, k_cmp, v_cmp_t, dk, vt_all, st):
    bsz, n, _ = dq.shape
    nc = n // CMP_STRIDE
    nsel = n // SEL_BLOCK
    return pl.pallas_call(
        _nsa_kernel,
        grid=(bsz, n // Q_TILE),
        in_specs=[pl.BlockSpec((1, Q_TILE, GROUP_WIDTH), lambda b, i: (b, i, 0)),
                  pl.BlockSpec((1, nc, LANES), lambda b, i: (b, 0, 0)),
                  pl.BlockSpec((1, HEAD_DIM, nc), lambda b, i: (b, 0, 0)),
                  pl.BlockSpec((1, n, LANES), lambda b, i: (b, 0, 0)),
                  pl.BlockSpec((1, HEAD_DIM, n), lambda b, i: (b, 4, 0)),
                  pl.BlockSpec((1, HEAD_DIM, n), lambda b, i: (b, 5, 0)),
                  pl.BlockSpec((1, 16, Q_TILE), lambda b, i: (b, 0, i)),
                  pl.BlockSpec((nsel, nc), lambda b, i: (0, 0))],
        out_specs=pl.BlockSpec((1, Q_TILE, GROUP_WIDTH), lambda b, i: (b, i, 0)),
        out_shape=jax.ShapeDtypeStruct((bsz, n, GROUP_WIDTH), BF16),
        scratch_shapes=[pltpu.VMEM((nsel, Q_TILE), F32)],
        compiler_params=_cparams(("parallel", "arbitrary")),
        name="nsa_attention",
    )(dq, k_cmp, v_cmp_t, dk, vt_all, vt_all, st, _overlap_matrix(n))


def _gmlp_pool_kernel(u_ref, v_ref, z_ref, zp_ref, ng_ref, ws_ref, bs_ref, wp_ref, ps_ref, ob_ref, oc_ref):
    i = pl.program_id(1)
    chunk = u_ref.shape[1]
    lane = lax.broadcasted_iota(I32, (1, GROUP_WIDTH), 1)

    u = _gelu(u_ref[0])
    v = _gelu(v_ref[0])
    mu = jnp.mean(v, axis=-1, keepdims=True)
    vc = v - mu
    var = jnp.mean(vc * vc, axis=-1, keepdims=True)
    vn = (vc * lax.rsqrt(var + EPS) * ng_ref[...]).astype(BF16)
    row = lax.broadcasted_iota(I32, (chunk, chunk), 0)
    colm = lax.broadcasted_iota(I32, (chunk, chunk), 1)
    mixed = bs_ref[...]
    for h in range(GROUP_HEADS):
        w = jnp.where(colm <= row, ws_ref[h], 0.0).astype(BF16)
        mixed = mixed + jnp.where((lane // HEAD_DIM) == h, _dot(w, vn), 0.0)
    ob_ref[0] = (u * mixed).astype(BF16)

    z = z_ref[0]
    prev = jnp.where(i > 0, zp_ref[0, chunk - 16:chunk, :], 0.0)
    rows = chunk + 16
    acc = jnp.concatenate([prev, z], axis=0)
    t_col = i * chunk + lax.broadcasted_iota(I32, (chunk, 1), 0)
    pooled = jnp.zeros((chunk, GROUP_WIDTH), F32)
    pool_ch = GROUP_WIDTH // len(POOL_WINDOWS)
    shift = 1
    for g, w in enumerate(POOL_WINDOWS):
        acc = acc + pltpu.roll(acc, shift, 0)
        shift *= 2
        count = jnp.minimum(t_col + 1, w).astype(F32)
        pooled = jnp.where((lane // pool_ch) == g, acc[16:rows] / count, pooled)
    pooled = pooled - z
    oc_ref[0] = (_dot(pooled.astype(BF16), wp_ref[...]) * ps_ref[...]).astype(BF16)


def _gmlp_pool_call(u, v, cz, sgu_g, w_sgu, b_sgu, w_pool, pool_scale):
    bsz, n, _ = u.shape
    chunk = Q_TILE
    bias = jnp.repeat(b_sgu.T, HEAD_DIM, axis=1)
    ngrp = len(POOL_WINDOWS)
    pc = GROUP_WIDTH // ngrp
    wbd = jnp.zeros((GROUP_WIDTH, GROUP_WIDTH), F32)
    for g in range(ngrp):
        wbd = wbd.at[g * pc:(g + 1) * pc, g * pc:(g + 1) * pc].set(w_pool[g])
    tok = pl.BlockSpec((1, chunk, GROUP_WIDTH), lambda b, i: (b, i, 0))
    const = lambda shape: pl.BlockSpec(shape, lambda b, i: tuple(0 for _ in shape))
    out = jax.ShapeDtypeStruct((bsz, n, GROUP_WIDTH), BF16)
    return pl.pallas_call(
        _gmlp_pool_kernel,
        grid=(bsz, n // chunk),
        in_specs=[tok, tok, tok,
                  pl.BlockSpec((1, chunk, GROUP_WIDTH), lambda b, i: (b, jnp.maximum(i - 1, 0), 0)),
                  const((1, GROUP_WIDTH)), const((GROUP_HEADS, chunk, chunk)), const((chunk, GROUP_WIDTH)),
                  const((GROUP_WIDTH, GROUP_WIDTH)), const((1, GROUP_WIDTH))],
        out_specs=[tok, tok],
        out_shape=[out, out],
        compiler_params=_cparams(("parallel", "arbitrary")),
        name="gmlp_pool",
    )(u, v, cz, cz, sgu_g.reshape(1, -1), w_sgu, bias, wbd.astype(BF16), pool_scale.reshape(1, -1))


def _ffn_kernel(oa_ref, ob_ref, oc_ref, od_ref, x_ref, mod_ref, wo_ref, g_ref, rwt_ref, rb_ref,
                wg_ref, wu_ref, wd_ref, fg_ref, out_ref, x1_ref, h_ref, acc_ref, gate_ref, *, final):
    s = pl.program_id(1)
    tm = x_ref.shape[0]

    @pl.when(s == 0)
    def _():
        mixed = _dot(oa_ref[...], wo_ref[0:256, :])
        mixed = mixed + _dot(ob_ref[...], wo_ref[256:512, :])
        mixed = mixed + _dot(oc_ref[...], wo_ref[512:768, :])
        mixed = mixed + _dot(od_ref[...], wo_ref[768:1024, :])
        x1 = x_ref[...] + mod_ref[0, 2:3, :] * mixed
        x1_ref[...] = x1
        ms = jnp.mean(x1 * x1, axis=-1, keepdims=True)
        h = x1 * lax.rsqrt(ms + EPS) * g_ref[...] * (1.0 + mod_ref[0, 4:5, :]) + mod_ref[0, 3:4, :]
        h_ref[...] = h.astype(BF16)
        acc_ref[...] = jnp.zeros_like(acc_ref)

        aff = _sigmoid(_dot_nt(rwt_ref[...], h, precision=HIGHEST))
        biased = aff + rb_ref[...]
        r = lax.broadcasted_iota(I32, (N_EXPERTS, 1), 0)
        best_score = None
        best_group = None
        for g in range(N_EXPERT_GROUPS):
            b = [biased[EXPERTS_PER_GROUP * g + e:EXPERTS_PER_GROUP * g + e + 1, :] for e in range(EXPERTS_PER_GROUP)]
            top1 = jnp.maximum(jnp.maximum(b[0], b[1]), jnp.maximum(b[2], b[3]))
            pair_mins = [jnp.minimum(b[a], b[c]) for a in range(4) for c in range(a + 1, 4)]
            top2 = functools.reduce(jnp.maximum, pair_mins)
            score = top1 + top2
            if g == 0:
                best_score, best_group = score, jnp.zeros_like(score, dtype=I32)
            else:
                better = score > best_score
                best_score = jnp.where(better, score, best_score)
                best_group = jnp.where(better, g, best_group)
        cand = jnp.where((r // EXPERTS_PER_GROUP) == best_group, biased, -jnp.inf)
        m1 = jnp.max(cand, axis=0, keepdims=True)
        i1 = jnp.min(jnp.where(cand == m1, r, N_EXPERTS), axis=0, keepdims=True)
        cand2 = jnp.where(r == i1, -jnp.inf, cand)
        m2 = jnp.max(cand2, axis=0, keepdims=True)
        i2 = jnp.min(jnp.where(cand2 == m2, r, N_EXPERTS), axis=0, keepdims=True)
        a1 = jnp.sum(jnp.where(r == i1, aff, 0.0), axis=0, keepdims=True)
        a2 = jnp.sum(jnp.where(r == i2, aff, 0.0), axis=0, keepdims=True)
        tot = a1 + a2
        gate_t = jnp.where(r == i1, a1 / tot, 0.0) + jnp.where(r == i2, a2 / tot, 0.0)
        gate_tok = jnp.concatenate([gate_t, jnp.zeros((LANES - N_EXPERTS, tm), F32)], axis=0).T
        gate_ref[0] = gate_tok
        for c in range(1, N_EXPERT_GROUPS):
            gate_ref[c] = pltpu.roll(gate_tok, LANES - EXPERTS_PER_GROUP * c, 1)

    hb = h_ref[...]
    hid = _dot(hb, wg_ref[...])
    hid = hid * _sigmoid(hid) * _dot(hb, wu_ref[...])
    g4 = gate_ref[s]
    gexp = jnp.concatenate([jnp.broadcast_to(g4[:, e:e + 1], (tm, D_EXPERT)) for e in range(EXPERTS_PER_GROUP)], axis=1)
    acc_ref[...] += _dot((hid * gexp).astype(BF16), wd_ref[...])

    @pl.when(s == N_EXPERT_GROUPS - 1)
    def _():
        y = x1_ref[...] + mod_ref[0, 5:6, :] * acc_ref[...]
        if final:
            ms = jnp.mean(y * y, axis=-1, keepdims=True)
            y = y * lax.rsqrt(ms + EPS) * fg_ref[...]
        out_ref[...] = y


def _ffn_call(oa, ob, oc, od, x, mod, w_out, g_ffn, router_w, router_b, w_gate, w_up, w_down, final_g, final):
    bsz, n, d = x.shape
    tm = TOKEN_TILE
    ntok = bsz * n
    tiles_per_batch = n // tm
    flat = lambda a: a.reshape(ntok, a.shape[-1])
    gw = EXPERTS_PER_GROUP * D_EXPERT
    wg = w_gate.transpose(1, 0, 2).reshape(d, N_EXPERTS * D_EXPERT).astype(BF16)
    wu = w_up.transpose(1, 0, 2).reshape(d, N_EXPERTS * D_EXPERT).astype(BF16)
    wd = w_down.reshape(N_EXPERTS * D_EXPERT, d).astype(BF16)
    tok = lambda width: pl.BlockSpec((tm, width), lambda i, s: (i, 0))
    const = lambda shape: pl.BlockSpec(shape, lambda i, s: (0, 0))
    out = pl.pallas_call(
        functools.partial(_ffn_kernel, final=final),
        grid=(ntok // tm, N_EXPERT_GROUPS),
        in_specs=[tok(GROUP_WIDTH)] * 4 + [
            tok(d), pl.BlockSpec((1, 8, d), lambda i, s: (i // tiles_per_batch, 0, 0)),
            const((d, d)), const((1, d)), const((N_EXPERTS, d)), const((N_EXPERTS, 1)),
            pl.BlockSpec((d, gw), lambda i, s: (0, s)), pl.BlockSpec((d, gw), lambda i, s: (0, s)),
            pl.BlockSpec((gw, d), lambda i, s: (s, 0)), const((1, d))],
        out_specs=pl.BlockSpec((tm, d), lambda i, s: (i, 0)),
        out_shape=jax.ShapeDtypeStruct((ntok, d), F32),
        scratch_shapes=[pltpu.VMEM((tm, d), F32), pltpu.VMEM((tm, d), BF16), pltpu.VMEM((tm, d), F32),
                        pltpu.VMEM((N_EXPERT_GROUPS, tm, LANES), F32)],
        compiler_params=_cparams(("parallel", "arbitrary")),
        name="outproj_moe",
    )(flat(oa), flat(ob), flat(oc), flat(od), flat(x), mod, w_out.astype(BF16), g_ffn.reshape(1, d),
      router_w.T, router_b.reshape(N_EXPERTS, 1), wg, wu, wd, final_g.reshape(1, d))
    return out.reshape(bsz, n, d)


def kernel(x, c, positions, ada_w, ada_b, norm_mix_g, norm_ffn_g, w_in, sgu_norm_g, w_sgu, b_sgu, w_pool,
           pool_scale, w_cmp1, w_cmp2, cmp_pe, w_out, router_w, router_b, w_gate, w_up, w_down, final_norm_g):
    depth = ada_w.shape[0]
    mod = _adaln_mod(c, ada_w, ada_b)
    tables = _rope_call(positions)
    for layer in range(depth):
        (aq, ak, dq, dk, iq, ik4, u, v, cz, dc, vt_all, st) = _inproj_call(
            x, mod[layer], norm_mix_g[layer], w_in[layer], tables)
        k_cmp, v_cmp_t = _cmp_call(dc, w_cmp1[layer], w_cmp2[layer], cmp_pe[layer], positions)
        o_a = _dsa_call(aq, ak, vt_all, iq, ik4, st)
        o_b, o_c = _gmlp_pool_call(u, v, cz, sgu_norm_g[layer], w_sgu[layer], b_sgu[layer], w_pool[layer],
                                   pool_scale[layer])
        o_d = _nsa_call(dq, k_cmp, v_cmp_t, dk, vt_all, st)
        x = _ffn_call(o_a, o_b, o_c, o_d, x, mod[layer], w_out[layer], norm_ffn_g[layer], router_w, router_b,
                      w_gate[layer], w_up[layer], w_down[layer], final_norm_g, final=(layer == depth - 1))
    return x
```

```python
import functools
import math

import numpy as np
import jax
import jax.numpy as jnp
from jax import lax
from jax.experimental import pallas as pl
from jax.experimental.pallas import tpu as pltpu

F32 = jnp.float32
BF16 = jnp.bfloat16
I32 = jnp.int32
HIGHEST = lax.Precision.HIGHEST

LANES = 128
HEAD_DIM = 64
GROUP_WIDTH = 256
GROUP_HEADS = 4
ROPE_THETA = 500000.0
EPS = 1e-6
NEG = -1e30
IDX_DIM = 32
DSA_TOPK = 256
CMP_BLOCK = 32
CMP_STRIDE = 16
SEL_BLOCK = 64
SEL_TOPN = 16
WINDOW = 512
N_EXPERTS = 16
EXPERTS_PER_GROUP = 4
N_EXPERT_GROUPS = 4
D_EXPERT = 256
POOL_WINDOWS = (2, 4, 8, 16)

Q_TILE = 128
KEY_CHUNK = 512
ATTN_BLOCK = 256
TOKEN_TILE = 512
VMEM_LIMIT = 56 * 1024 * 1024

INT_MIN = -2 ** 31
QK_SCALE = HEAD_DIM ** -0.5 * math.log2(math.e)
IQ_PARTS = ("hi", "lo", "hi", "zero")
IK_PARTS = ("hi", "hi", "lo", "zero")

_OFF = {}
_o = 0
for _name, _w in (("a_q", 256), ("a_k", 256), ("a_v", 256), ("a_iq", 128), ("a_ik", 32), ("a_iw", 4),
                  ("b_u", 256), ("b_v", 256), ("c_z", 256), ("d_q", 256), ("d_kc", 64), ("d_vc", 64),
                  ("d_ks", 64), ("d_vs", 64), ("d_kw", 64), ("d_vw", 64), ("d_g", 12)):
    _OFF[_name] = (_o, _o + _w)
    _o += _w
ROW_WIDTH = 2432
COL_ROWS = 400


def _cparams(sem):
    return pltpu.CompilerParams(dimension_semantics=sem, vmem_limit_bytes=VMEM_LIMIT)


def _sigmoid(x):
    return 1.0 / (1.0 + jnp.exp(-x))


def _gelu(x):
    return 0.5 * x * (1.0 + jnp.tanh(0.7978845608028654 * (x + 0.044715 * (x * x * x))))


def _split_hi_lo(x, parts):
    hi = x.astype(BF16).astype(F32)
    slot = lax.broadcasted_iota(I32, (1, LANES), 1) // IDX_DIM
    out = jnp.zeros_like(x)
    for j, part in enumerate(parts):
        if part != "zero":
            out = jnp.where(slot == j, hi if part == "hi" else x - hi, out)
    return out.astype(BF16)


def _dot(a, b, **kw):
    return jnp.dot(a, b, preferred_element_type=F32, **kw)


def _dot_nt(a, b, **kw):
    return lax.dot_general(a, b, (((1,), (1,)), ((), ())), preferred_element_type=F32, **kw)


def _mod_kernel(c_ref, w_ref, b_ref, o_ref):
    c = c_ref[...]
    ca = c * _sigmoid(c)
    o_ref[0] = _dot(ca, w_ref[0], precision=HIGHEST) + b_ref[0]


def _adaln_mod(c, ada_w, ada_b):
    depth, d, d6 = ada_w.shape
    bsz = c.shape[0]
    rows = 8
    c_pad = jnp.zeros((rows, d), F32).at[:bsz].set(c)
    tn = 1536
    out = pl.pallas_call(
        _mod_kernel,
        grid=(depth, d6 // tn),
        in_specs=[pl.BlockSpec((rows, d), lambda l, j: (0, 0)),
                  pl.BlockSpec((1, d, tn), lambda l, j: (l, 0, j)),
                  pl.BlockSpec((1, 1, tn), lambda l, j: (l, 0, j))],
        out_specs=pl.BlockSpec((1, rows, tn), lambda l, j: (l, 0, j)),
        out_shape=jax.ShapeDtypeStruct((depth, rows, d6), F32),
        compiler_params=_cparams(("arbitrary", "arbitrary")),
        name="adaln_mod",
    )(c_pad, ada_w, ada_b.reshape(depth, 1, d6))
    mod = out[:, :bsz].reshape(depth, bsz, 6, d)
    return jnp.concatenate([mod, jnp.zeros((depth, bsz, 2, d), F32)], axis=2)


def _freq_row(hd):
    rd = hd // 4
    half = rd // 2
    inv = (ROPE_THETA ** (-np.arange(half, dtype=np.float32) / np.float32(half))).astype(np.float32)
    row = np.zeros((1, LANES), np.float32)
    for lane in range(LANES):
        r = lane % hd
        if r < rd:
            row[0, lane] = inv[r % half]
    return jnp.asarray(row)


def _rope_tables(pos_col, frow, hd):
    half = hd // 8
    ang = pos_col * frow
    c = jnp.cos(ang)
    s = jnp.sin(ang)
    r = lax.broadcasted_iota(I32, (1, LANES), 1) % hd
    sa = jnp.where(r < half, -s, 0.0)
    sb = jnp.where((r >= half) & (r < 2 * half), s, 0.0)
    return c, sa, sb


def _rope_kernel(pos_ref, f64_ref, f32_ref, c64, sa64, sb64, c32, sa32, sb32):
    pos = pos_ref[0].astype(F32)
    c, sa, sb = _rope_tables(pos, f64_ref[...], HEAD_DIM)
    c64[0] = c
    sa64[0] = sa
    sb64[0] = sb
    c, sa, sb = _rope_tables(pos, f32_ref[...], IDX_DIM)
    c32[0] = c
    sa32[0] = sa
    sb32[0] = sb


def _rope_call(positions):
    bsz, n = positions.shape
    tm = TOKEN_TILE
    tab = jax.ShapeDtypeStruct((bsz, n, LANES), F32)
    tspec = pl.BlockSpec((1, tm, LANES), lambda b, i: (b, i, 0))
    fspec = pl.BlockSpec((1, LANES), lambda b, i: (0, 0))
    return pl.pallas_call(
        _rope_kernel,
        grid=(bsz, n // tm),
        in_specs=[pl.BlockSpec((1, tm, 1), lambda b, i: (b, i, 0)), fspec, fspec],
        out_specs=[tspec] * 6,
        out_shape=[tab] * 6,
        compiler_params=_cparams(("parallel", "arbitrary")),
        name="rope_tables",
    )(positions.reshape(bsz, n, 1), _freq_row(HEAD_DIM), _freq_row(IDX_DIM))


def _apply_rope(x, c, sa, sb, half):
    return x * c + pltpu.roll(x, LANES - half, 1) * sa + pltpu.roll(x, half, 1) * sb


def _inproj_kernel(x_ref, mod_ref, g_ref, w_ref, wt_ref, c64, sa64, sb64, c32, sa32, sb32,
                   aq_ref, ak_ref, dq_ref, dk_ref, iq_ref, ik_ref, u_ref, v_ref, cz_ref, dc_ref,
                   vt_ref, st_ref):
    x = x_ref[0]
    ms = jnp.mean(x * x, axis=-1, keepdims=True)
    y = x * lax.rsqrt(ms + EPS) * g_ref[...]
    h = y * (1.0 + mod_ref[0, 1:2, :]) + mod_ref[0, 0:1, :]
    hb = h.astype(BF16)
    z = _dot(hb, w_ref[...])
    zt = _dot_nt(wt_ref[...], hb)

    c, sa, sb = c64[0], sa64[0], sb64[0]
    tiles = [_apply_rope(z[:, LANES * t:LANES * (t + 1)], c, sa, sb, 8) for t in range(7)]
    aq_ref[0] = jnp.concatenate([tiles[0] * QK_SCALE, tiles[1] * QK_SCALE], axis=1).astype(BF16)
    ak_ref[0] = jnp.concatenate([tiles[2], tiles[3]], axis=1).astype(BF16)
    dq_ref[0] = jnp.concatenate([tiles[4] * QK_SCALE, tiles[5] * QK_SCALE], axis=1)
    dk_ref[0] = tiles[6].astype(BF16)
    c, sa, sb = c32[0], sa32[0], sb32[0]
    iq_ref[0] = jnp.concatenate(
        [_split_hi_lo(_apply_rope(z[:, 896 + LANES * h:1024 + LANES * h], c, sa, sb, 4), IQ_PARTS) for h in range(4)],
        axis=1)
    ik_ref[0] = _split_hi_lo(_apply_rope(z[:, 1408:1536], c, sa, sb, 4), IK_PARTS)
    u_ref[0] = z[:, 1536:1792]
    v_ref[0] = z[:, 1792:2048]
    cz_ref[0] = z[:, 2048:2304]
    dc_ref[0] = z[:, 2304:2432]
    vt_ref[0] = zt[0:384].astype(BF16)
    st_ref[0] = zt[384:400]


def _pack_w_in(w_in):
    def col(name):
        a, b = _OFF[name]
        return w_in[:, a:b]
    iq = col("a_iq")
    zero = jnp.zeros((w_in.shape[0], IDX_DIM), w_in.dtype)
    iq_cols = []
    for h in range(4):
        w_h = iq[:, IDX_DIM * h:IDX_DIM * (h + 1)]
        iq_cols += [w_h, w_h, w_h, zero]
    w_row = jnp.concatenate([col("a_q"), col("a_k"), col("d_q"), col("d_ks"), col("d_kw")] + iq_cols +
                            [col("a_ik"), col("a_ik"), col("a_ik"), col("a_ik"),
                             col("b_u"), col("b_v"), col("c_z"), col("d_kc"), col("d_vc")], axis=1)
    w_col = jnp.concatenate([col("a_v"), col("d_vs"), col("d_vw"), col("a_iw"), col("d_g")], axis=1).T
    return w_row.astype(BF16), w_col.astype(BF16)


def _inproj_call(x, mod, g, w_in, tables):
    bsz, n, d = x.shape
    tm = TOKEN_TILE
    w_row, w_col = _pack_w_in(w_in)
    tok = lambda width: pl.BlockSpec((1, tm, width), lambda b, i: (b, i, 0))
    shp = lambda width, dt: jax.ShapeDtypeStruct((bsz, n, width), dt)
    const2 = lambda shape: pl.BlockSpec(shape, lambda b, i: (0, 0))
    return pl.pallas_call(
        _inproj_kernel,
        grid=(bsz, n // tm),
        in_specs=[tok(d), pl.BlockSpec((1, 8, d), lambda b, i: (b, 0, 0)), const2((1, d)),
                  const2((d, ROW_WIDTH)), const2((COL_ROWS, d))] + [tok(LANES)] * 6,
        out_specs=[tok(256), tok(256), tok(256), tok(128), tok(512), tok(128), tok(256), tok(256), tok(256),
                   tok(128),
                   pl.BlockSpec((1, 384, tm), lambda b, i: (b, 0, i)),
                   pl.BlockSpec((1, 16, tm), lambda b, i: (b, 0, i))],
        out_shape=[shp(256, BF16), shp(256, BF16), shp(256, F32), shp(128, BF16), shp(512, BF16), shp(128, BF16),
                   shp(256, F32), shp(256, F32), shp(256, F32), shp(128, F32),
                   jax.ShapeDtypeStruct((bsz, 384, n), BF16),
                   jax.ShapeDtypeStruct((bsz, 16, n), F32)],
        compiler_params=_cparams(("parallel", "arbitrary")),
        name="inproj",
    )(x, mod, g.reshape(1, d), w_row, w_col, *tables)


def _cmp_kernel(kf_ref, vf_ref, w1a_ref, w1b_ref, w1_ref, pe_ref, w2k_ref, w2vt_ref, pos_ref, f_ref,
                kc_ref, vct_ref):
    nc = kf_ref.shape[1]

    def hidden(flat, idx):
        fb = flat.astype(BF16)
        a = _dot(fb, w1a_ref[idx])
        b = _dot(fb, w1b_ref[idx])
        b_next = pltpu.roll(b, nc - 1, 0)
        bias = _dot(jnp.broadcast_to(pe_ref[idx], (8, pe_ref.shape[2])), w1_ref[idx], precision=HIGHEST)[0:1]
        return _gelu(a + b_next + bias).astype(BF16)

    k_c = _dot(hidden(kf_ref[0], 0), w2k_ref[...])
    c, sa, sb = _rope_tables(pos_ref[0].astype(F32), f_ref[...], HEAD_DIM)
    kc_ref[0] = _apply_rope(k_c, c, sa, sb, 8).astype(BF16)
    vct_ref[0] = _dot_nt(w2vt_ref[...], hidden(vf_ref[0], 1)).astype(BF16)


def _cmp_call(dc, w_cmp1, w_cmp2, cmp_pe, positions):
    bsz, n, _ = dc.shape
    nc = n // CMP_STRIDE
    half = CMP_STRIDE * HEAD_DIM
    flat = dc.reshape(bsz, nc, CMP_STRIDE, 2, HEAD_DIM).transpose(3, 0, 1, 2, 4).reshape(2, bsz, nc, half)
    w1 = w_cmp1.reshape(2, CMP_BLOCK * HEAD_DIM, HEAD_DIM)
    w1a = w1[:, :half].astype(BF16)
    w1b = w1[:, half:].astype(BF16)
    pe = cmp_pe.reshape(2, 1, CMP_BLOCK * HEAD_DIM)
    w2k = jnp.concatenate([w_cmp2[0], jnp.zeros_like(w_cmp2[0])], axis=1).astype(BF16)
    w2vt = w_cmp2[1].T.astype(BF16)
    mid = jnp.minimum(jnp.arange(nc) * CMP_STRIDE + CMP_BLOCK // 2, n - 1)
    pos_c = positions[:, mid].reshape(bsz, nc, 1)
    per_b = lambda shape: pl.BlockSpec(shape, lambda b: (b, 0, 0))
    const = lambda shape: pl.BlockSpec(shape, lambda b: tuple(0 for _ in shape))
    return pl.pallas_call(
        _cmp_kernel,
        grid=(bsz,),
        in_specs=[per_b((1, nc, half)), per_b((1, nc, half)), const((2, half, HEAD_DIM)), const((2, half, HEAD_DIM)),
                  const((2, 2 * half, HEAD_DIM)), const((2, 1, 2 * half)), const((HEAD_DIM, LANES)),
                  const((HEAD_DIM, HEAD_DIM)), per_b((1, nc, 1)), const((1, LANES))],
        out_specs=[per_b((1, nc, LANES)), per_b((1, HEAD_DIM, nc))],
        out_shape=[jax.ShapeDtypeStruct((bsz, nc, LANES), BF16), jax.ShapeDtypeStruct((bsz, HEAD_DIM, nc), BF16)],
        compiler_params=_cparams(("arbitrary",)),
        name="nsa_compress",
    )(flat[0], flat[1], w1a, w1b, w1, pe, w2k, w2vt, pos_c, _freq_row(HEAD_DIM))


def _head_masked(q, h, width):
    lane = lax.broadcasted_iota(I32, (1, width), 1)
    return jnp.where((lane // HEAD_DIM) == h, q, jnp.zeros_like(q))


def _online_step(state, s, bias, vt):
    m, l, acc = state
    s = s + bias
    m_new = jnp.maximum(m, jnp.max(s, axis=0, keepdims=True))
    alpha = jnp.exp2(m - m_new)
    p = jnp.exp2(s - m_new)
    l = alpha * l + jnp.sum(p, axis=0, keepdims=True)
    acc = alpha * acc + _dot(vt, p.astype(BF16))
    return m_new, l, acc


def _init_state():
    return (jnp.full((1, Q_TILE), NEG, F32), jnp.zeros((1, Q_TILE), F32), jnp.zeros((HEAD_DIM, Q_TILE), F32))


def _float_key(x):
    bits = pltpu.bitcast(x + 0.0, I32)
    return bits ^ ((bits >> 31) & jnp.int32(0x7FFFFFFF))


def _radix_kth(count_ge, k):
    def bit_body(bi, thr):
        cand = thr ^ (jnp.full((1, Q_TILE), 1, I32) << (31 - bi))
        return jnp.where(count_ge(cand) >= k, cand, thr)
    return lax.fori_loop(0, 32, bit_body, jnp.full((1, Q_TILE), INT_MIN, I32))


def _dsa_kernel(q_ref, k_ref, vt_ref, iq_ref, ik_ref, st_ref, tri_ref, o_ref, keys_ref):
    i = pl.program_id(1)
    ck = KEY_CHUNK
    ab = ATTN_BLOCK
    nck = (i * Q_TILE + Q_TILE + ck - 1) // ck
    nblk = (i * Q_TILE + Q_TILE + ab - 1) // ab
    t_row = i * Q_TILE + lax.broadcasted_iota(I32, (1, Q_TILE), 1)

    iqp = iq_ref[0]
    iq_stack = jnp.concatenate([iqp[:, LANES * h:LANES * (h + 1)] for h in range(4)], axis=0)
    iw = [st_ref[0, h:h + 1, :] for h in range(4)]

    sub = Q_TILE
    sub_col = lax.broadcasted_iota(I32, (sub, 1), 0)

    def score_body(c, _):
        for r in range(ck // sub):
            off = pl.multiple_of(c * ck + r * sub, sub)
            s_all = _dot_nt(ik_ref[0, pl.ds(off, sub), :], iq_stack)
            sc = iw[0] * jnp.maximum(s_all[:, 0:LANES], 0.0)
            for h in range(1, 4):
                sc = sc + iw[h] * jnp.maximum(s_all[:, LANES * h:LANES * (h + 1)], 0.0)
            keys_ref[pl.ds(off, sub), :] = jnp.where(off + sub_col <= t_row, _float_key(sc), INT_MIN)
        return 0

    lax.fori_loop(0, nck, score_body, 0)

    def count(pred):
        def body(c, acc):
            kc = keys_ref[pl.ds(pl.multiple_of(c * ck, ck), ck), :]
            return acc + jnp.sum(jnp.where(pred(kc), 1, 0).astype(I32).reshape(ck // 8, 8, Q_TILE), axis=0)
        acc = lax.fori_loop(0, nck, body, jnp.zeros((8, Q_TILE), I32))
        return jnp.sum(acc, axis=0, keepdims=True)

    thr = _radix_kth(lambda cand: count(lambda kc: kc >= cand), DSA_TOPK)
    n_ge = count(lambda kc: kc >= thr)
    need = (DSA_TOPK - count(lambda kc: kc > thr)).astype(F32)
    excess_ties = jnp.max(jnp.where((n_ge > DSA_TOPK) & (thr > INT_MIN), 1, 0)) > 0
    thr_floor = jnp.maximum(thr, INT_MIN + 1)

    q = q_ref[0]
    q_stack = jnp.concatenate([_head_masked(q, h, GROUP_WIDTH) for h in range(4)], axis=0)
    blk_col = lax.broadcasted_iota(I32, (ab, 1), 0)

    def attend(rank_ties):
        def body(c, carry):
            states, tie_carry = carry
            off = pl.multiple_of(c * ab, ab)
            kc = keys_ref[pl.ds(off, ab), :]
            if rank_ties:
                eq = kc == thr
                eq_f = jnp.where(eq, 1.0, 0.0)
                rank = _dot(tri_ref[0:ab, 0:ab], eq_f.astype(BF16)) + tie_carry
                sel = ((kc > thr) | (eq & (rank < need))) & (off + blk_col <= t_row)
                tie_carry = tie_carry + jnp.sum(eq_f, axis=0, keepdims=True)
            else:
                sel = kc >= thr_floor
            bias = jnp.where(sel, 0.0, NEG)
            s_all = _dot_nt(k_ref[0, pl.ds(off, ab), :], q_stack)
            states = tuple(
                _online_step(states[h], s_all[:, LANES * h:LANES * (h + 1)], bias,
                             vt_ref[0, h * HEAD_DIM:(h + 1) * HEAD_DIM, pl.ds(off, ab)]) for h in range(4))
            return states, tie_carry

        init = (tuple(_init_state() for _ in range(4)), jnp.zeros((1, Q_TILE), F32))
        return lax.fori_loop(0, nblk, body, init)[0]

    states = lax.cond(excess_ties, lambda: attend(True), lambda: attend(False))
    out_t = jnp.concatenate([acc / l for (_, l, acc) in states], axis=0)
    o_ref[0] = out_t.T.astype(BF16)


def _strict_lower(size):
    return jnp.asarray(np.tril(np.ones((size, size), np.float32), -1)).astype(BF16)


def _dsa_call(aq, ak, vt_all, iqp, ikp, st):
    bsz, n, _ = aq.shape
    return pl.pallas_call(
        _dsa_kernel,
        grid=(bsz, n // Q_TILE),
        in_specs=[pl.BlockSpec((1, Q_TILE, GROUP_WIDTH), lambda b, i: (b, i, 0)),
                  pl.BlockSpec((1, n, GROUP_WIDTH), lambda b, i: (b, 0, 0)),
                  pl.BlockSpec((1, GROUP_WIDTH, n), lambda b, i: (b, 0, 0)),
                  pl.BlockSpec((1, Q_TILE, 4 * LANES), lambda b, i: (b, i, 0)),
                  pl.BlockSpec((1, n, LANES), lambda b, i: (b, 0, 0)),
                  pl.BlockSpec((1, 16, Q_TILE), lambda b, i: (b, 0, i)),
                  pl.BlockSpec((ATTN_BLOCK, ATTN_BLOCK), lambda b, i: (0, 0))],
        out_specs=pl.BlockSpec((1, Q_TILE, GROUP_WIDTH), lambda b, i: (b, i, 0)),
        out_shape=jax.ShapeDtypeStruct((bsz, n, GROUP_WIDTH), BF16),
        scratch_shapes=[pltpu.VMEM((n, Q_TILE), I32)],
        compiler_params=_cparams(("parallel", "arbitrary")),
        name="dsa_attention",
    )(aq, ak, vt_all, iqp, ikp, st, _strict_lower(ATTN_BLOCK))


def _nsa_kernel(q_ref, kc_ref, vct_ref, dk_ref, vst_ref, vwt_ref, st_ref, ov_ref, tri_ref, o_ref, sel_ref):
    i = pl.program_id(1)
    nc = kc_ref.shape[1]
    nsel = ov_ref.shape[0]
    ab = ATTN_BLOCK
    nblk = (i * Q_TILE + Q_TILE + ab - 1) // ab
    t_row = i * Q_TILE + lax.broadcasted_iota(I32, (1, Q_TILE), 1)
    lane = lax.broadcasted_iota(I32, (1, LANES), 1)

    q = q_ref[0]
    q_sel, q_win = [], []
    for h in range(4):
        tile = q[:, LANES * (h // 2):LANES * (h // 2 + 1)]
        swapped = pltpu.roll(tile, HEAD_DIM, 1)
        low, high = (tile, swapped) if h % 2 == 0 else (swapped, tile)
        q_sel.append(jnp.where(lane < HEAD_DIM, low, 0.0).astype(BF16))
        q_win.append(jnp.where(lane >= HEAD_DIM, high, 0.0).astype(BF16))

    qs_stack = jnp.concatenate(q_sel, axis=0)
    qw_stack = jnp.concatenate(q_win, axis=0)

    n_col = lax.broadcasted_iota(I32, (nc, 1), 0)
    vis_bias = jnp.where(n_col * CMP_STRIDE + (CMP_BLOCK - 1) <= t_row, 0.0, NEG)
    any_vis = t_row >= CMP_BLOCK - 1
    s_all = _dot_nt(kc_ref[0], qs_stack)
    o_cmp = []
    p_sum = jnp.zeros((nc, Q_TILE), F32)
    for h in range(4):
        s = s_all[:, LANES * h:LANES * (h + 1)] + vis_bias
        p = jnp.exp2(s - jnp.max(s, axis=0, keepdims=True))
        l = jnp.sum(p, axis=0, keepdims=True)
        p = p * jnp.where(any_vis, 1.0 / l, 0.0)
        p_sum = p_sum + p
        o_cmp.append(_dot(vct_ref[0], p.astype(BF16)))

    p_hi = p_sum.astype(BF16)
    p_lo = (p_sum - p_hi.astype(F32)).astype(BF16)
    imp = _dot(ov_ref[...], p_hi) + _dot(ov_ref[...], p_lo)
    j_col = lax.broadcasted_iota(I32, (nsel, 1), 0)
    forced = (j_col == t_row // SEL_BLOCK) | (j_col == 0)
    admissible = j_col * SEL_BLOCK <= t_row
    keys = jnp.where(admissible, _float_key(jnp.where(forced, jnp.inf, imp)), INT_MIN)

    def count(mask):
        return jnp.sum(jnp.sum(jnp.where(mask, 1, 0).astype(I32).reshape(nsel // 8, 8, Q_TILE), axis=0),
                       axis=0, keepdims=True)

    top_n = min(SEL_TOPN, nsel)
    thr = _radix_kth(lambda cand: count(keys >= cand), top_n)
    need = (top_n - count(keys > thr)).astype(F32)
    eq = keys == thr
    rank = _dot(tri_ref[...], jnp.where(eq, 1.0, 0.0).astype(BF16))
    chosen = ((keys > thr) | (eq & (rank < need))) & admissible
    sel_bias = jnp.where(chosen, 0.0, NEG)
    for j in range(nsel):
        sel_ref[j] = jnp.broadcast_to(sel_bias[j:j + 1, :], (8, Q_TILE))

    blk_col = lax.broadcasted_iota(I32, (ab, 1), 0)
    per_blk = ab // SEL_BLOCK

    def sel_body(c, states):
        off = pl.multiple_of(c * ab, ab)
        bias = jnp.concatenate([jnp.tile(sel_ref[c * per_blk + r], (SEL_BLOCK // 8, 1)) for r in range(per_blk)], axis=0)
        bias = jnp.where(off + blk_col <= t_row, bias, NEG)
        s_blk = _dot_nt(dk_ref[0, pl.ds(off, ab), :], qs_stack)
        vt = vst_ref[0, :, pl.ds(off, ab)]
        return tuple(_online_step(states[h], s_blk[:, LANES * h:LANES * (h + 1)], bias, vt) for h in range(4))

    st_sel = lax.fori_loop(0, nblk, sel_body, tuple(_init_state() for _ in range(4)))

    span = WINDOW + Q_TILE
    start = pl.multiple_of(jnp.maximum(i * Q_TILE - WINDOW, 0), Q_TILE)
    kpos = start + lax.broadcasted_iota(I32, (span, 1), 0)
    win_bias = jnp.where((kpos <= t_row) & (kpos > t_row - WINDOW), 0.0, NEG)
    s_win = _dot_nt(dk_ref[0, pl.ds(start, span), :], qw_stack)
    vw = vwt_ref[0, :, pl.ds(start, span)]
    st_win = [_online_step(_init_state(), s_win[:, LANES * h:LANES * (h + 1)], win_bias, vw) for h in range(4)]

    outs = []
    for h in range(4):
        g = [_sigmoid(st_ref[0, 4 + 3 * h + r:5 + 3 * h + r, :]) for r in range(3)]
        o_s = st_sel[h][2] / st_sel[h][1]
        o_w = st_win[h][2] / st_win[h][1]
        outs.append(g[0] * o_cmp[h] + g[1] * o_s + g[2] * o_w)
    o_ref[0] = jnp.concatenate(outs, axis=0).T.astype(BF16)


def _overlap_matrix(n):
    nc = n // CMP_STRIDE
    nsel = n // SEL_BLOCK
    sel_start = np.arange(nsel) * SEL_BLOCK
    cmp_start = np.arange(nc) * CMP_STRIDE
    ov = np.clip(np.minimum(sel_start[:, None] + SEL_BLOCK, cmp_start[None, :] + CMP_BLOCK)
                 - np.maximum(sel_start[:, None], cmp_start[None, :]), 0, None).astype(np.float32) / CMP_BLOCK
    ov[:, nc - 1] = 0.0
    return jnp.asarray(ov).astype(BF16)


def _nsa_call(dq, k_cmp, v_cmp_t, dk, vt_all, st):
    bsz, n, _ = dq.shape
    nc = n // CMP_STRIDE
    nsel = n // SEL_BLOCK
    return pl.pallas_call(
        _nsa_kernel,
        grid=(bsz, n // Q_TILE),
        in_specs=[pl.BlockSpec((1, Q_TILE, GROUP_WIDTH), lambda b, i: (b, i, 0)),
                  pl.BlockSpec((1, nc, LANES), lambda b, i: (b, 0, 0)),
                  pl.BlockSpec((1, HEAD_DIM, nc), lambda b, i: (b, 0, 0)),
                  pl.BlockSpec((1, n, LANES), lambda b, i: (b, 0, 0)),
                  pl.BlockSpec((1, HEAD_DIM, n), lambda b, i: (b, 4, 0)),
                  pl.BlockSpec((1, HEAD_DIM, n), lambda b, i: (b, 5, 0)),
                  pl.BlockSpec((1, 16, Q_TILE), lambda b, i: (b, 0, i)),
                  pl.BlockSpec((nsel, nc), lambda b, i: (0, 0)),
                  pl.BlockSpec((nsel, nsel), lambda b, i: (0, 0))],
        out_specs=pl.BlockSpec((1, Q_TILE, GROUP_WIDTH), lambda b, i: (b, i, 0)),
        out_shape=jax.ShapeDtypeStruct((bsz, n, GROUP_WIDTH), BF16),
        scratch_shapes=[pltpu.VMEM((nsel, 8, Q_TILE), F32)],
        compiler_params=_cparams(("parallel", "arbitrary")),
        name="nsa_attention",
    )(dq, k_cmp, v_cmp_t, dk, vt_all, vt_all, st, _overlap_matrix(n), _strict_lower(nsel))


def _gmlp_pool_kernel(u_ref, v_ref, z_ref, zp_ref, ng_ref, ws_ref, bs_ref, wp_ref, ps_ref, ob_ref, oc_ref):
    i = pl.program_id(1)
    chunk = u_ref.shape[1]
    lane = lax.broadcasted_iota(I32, (1, GROUP_WIDTH), 1)

    u = _gelu(u_ref[0])
    v = _gelu(v_ref[0])
    mu = jnp.mean(v, axis=-1, keepdims=True)
    vc = v - mu
    var = jnp.mean(vc * vc, axis=-1, keepdims=True)
    vn = (vc * lax.rsqrt(var + EPS) * ng_ref[...]).astype(BF16)
    row = lax.broadcasted_iota(I32, (chunk, chunk), 0)
    colm = lax.broadcasted_iota(I32, (chunk, chunk), 1)
    mixed = bs_ref[...]
    for h in range(GROUP_HEADS):
        w = jnp.where(colm <= row, ws_ref[h], 0.0).astype(BF16)
        mixed = mixed + jnp.where((lane // HEAD_DIM) == h, _dot(w, vn), 0.0)
    ob_ref[0] = (u * mixed).astype(BF16)

    z = z_ref[0]
    prev = jnp.where(i > 0, zp_ref[0, chunk - 16:chunk, :], 0.0)
    rows = chunk + 16
    acc = jnp.concatenate([prev, z], axis=0)
    t_col = i * chunk + lax.broadcasted_iota(I32, (chunk, 1), 0)
    pooled = jnp.zeros((chunk, GROUP_WIDTH), F32)
    pool_ch = GROUP_WIDTH // len(POOL_WINDOWS)
    shift = 1
    for g, w in enumerate(POOL_WINDOWS):
        acc = acc + pltpu.roll(acc, shift, 0)
        shift *= 2
        count = jnp.minimum(t_col + 1, w).astype(F32)
        pooled = jnp.where((lane // pool_ch) == g, acc[16:rows] / count, pooled)
    pooled = pooled - z
    oc_ref[0] = (_dot(pooled.astype(BF16), wp_ref[...]) * ps_ref[...]).astype(BF16)


def _gmlp_pool_call(u, v, cz, sgu_g, w_sgu, b_sgu, w_pool, pool_scale):
    bsz, n, _ = u.shape
    chunk = Q_TILE
    bias = jnp.repeat(b_sgu.T, HEAD_DIM, axis=1)
    ngrp = len(POOL_WINDOWS)
    pc = GROUP_WIDTH // ngrp
    wbd = jnp.zeros((GROUP_WIDTH, GROUP_WIDTH), F32)
    for g in range(ngrp):
        wbd = wbd.at[g * pc:(g + 1) * pc, g * pc:(g + 1) * pc].set(w_pool[g])
    tok = pl.BlockSpec((1, chunk, GROUP_WIDTH), lambda b, i: (b, i, 0))
    const = lambda shape: pl.BlockSpec(shape, lambda b, i: tuple(0 for _ in shape))
    out = jax.ShapeDtypeStruct((bsz, n, GROUP_WIDTH), BF16)
    return pl.pallas_call(
        _gmlp_pool_kernel,
        grid=(bsz, n // chunk),
        in_specs=[tok, tok, tok,
                  pl.BlockSpec((1, chunk, GROUP_WIDTH), lambda b, i: (b, jnp.maximum(i - 1, 0), 0)),
                  const((1, GROUP_WIDTH)), const((GROUP_HEADS, chunk, chunk)), const((chunk, GROUP_WIDTH)),
                  const((GROUP_WIDTH, GROUP_WIDTH)), const((1, GROUP_WIDTH))],
        out_specs=[tok, tok],
        out_shape=[out, out],
        compiler_params=_cparams(("parallel", "arbitrary")),
        name="gmlp_pool",
    )(u, v, cz, cz, sgu_g.reshape(1, -1), w_sgu, bias, wbd.astype(BF16), pool_scale.reshape(1, -1))


def _ffn_kernel(oa_ref, ob_ref, oc_ref, od_ref, x_ref, mod_ref, wo_ref, g_ref, rwt_ref, rb_ref,
                wg_ref, wu_ref, wd_ref, fg_ref, out_ref, x1_ref, h_ref, acc_ref, gate_ref, *, final):
    s = pl.program_id(1)
    tm = x_ref.shape[0]

    @pl.when(s == 0)
    def _():
        mixed = _dot(oa_ref[...], wo_ref[0:256, :])
        mixed = mixed + _dot(ob_ref[...], wo_ref[256:512, :])
        mixed = mixed + _dot(oc_ref[...], wo_ref[512:768, :])
        mixed = mixed + _dot(od_ref[...], wo_ref[768:1024, :])
        x1 = x_ref[...] + mod_ref[0, 2:3, :] * mixed
        x1_ref[...] = x1
        ms = jnp.mean(x1 * x1, axis=-1, keepdims=True)
        h = x1 * lax.rsqrt(ms + EPS) * g_ref[...] * (1.0 + mod_ref[0, 4:5, :]) + mod_ref[0, 3:4, :]
        h_ref[...] = h.astype(BF16)
        acc_ref[...] = jnp.zeros_like(acc_ref)

        aff = _sigmoid(_dot_nt(rwt_ref[...], h, precision=HIGHEST))
        biased = aff + rb_ref[...]
        r = lax.broadcasted_iota(I32, (N_EXPERTS, 1), 0)
        best_score = None
        best_group = None
        for g in range(N_EXPERT_GROUPS):
            b = [biased[EXPERTS_PER_GROUP * g + e:EXPERTS_PER_GROUP * g + e + 1, :] for e in range(EXPERTS_PER_GROUP)]
            top1 = jnp.maximum(jnp.maximum(b[0], b[1]), jnp.maximum(b[2], b[3]))
            pair_mins = [jnp.minimum(b[a], b[c]) for a in range(4) for c in range(a + 1, 4)]
            top2 = functools.reduce(jnp.maximum, pair_mins)
            score = top1 + top2
            if g == 0:
                best_score, best_group = score, jnp.zeros_like(score, dtype=I32)
            else:
                better = score > best_score
                best_score = jnp.where(better, score, best_score)
                best_group = jnp.where(better, g, best_group)
        cand = jnp.where((r // EXPERTS_PER_GROUP) == best_group, biased, -jnp.inf)
        m1 = jnp.max(cand, axis=0, keepdims=True)
        i1 = jnp.min(jnp.where(cand == m1, r, N_EXPERTS), axis=0, keepdims=True)
        cand2 = jnp.where(r == i1, -jnp.inf, cand)
        m2 = jnp.max(cand2, axis=0, keepdims=True)
        i2 = jnp.min(jnp.where(cand2 == m2, r, N_EXPERTS), axis=0, keepdims=True)
        a1 = jnp.sum(jnp.where(r == i1, aff, 0.0), axis=0, keepdims=True)
        a2 = jnp.sum(jnp.where(r == i2, aff, 0.0), axis=0, keepdims=True)
        tot = a1 + a2
        gate_t = jnp.where(r == i1, a1 / tot, 0.0) + jnp.where(r == i2, a2 / tot, 0.0)
        gate_tok = jnp.concatenate([gate_t, jnp.zeros((LANES - N_EXPERTS, tm), F32)], axis=0).T
        gate_ref[0] = gate_tok
        for c in range(1, N_EXPERT_GROUPS):
            gate_ref[c] = pltpu.roll(gate_tok, LANES - EXPERTS_PER_GROUP * c, 1)

    hb = h_ref[...]
    hid = _dot(hb, wg_ref[...])
    hid = hid * _sigmoid(hid) * _dot(hb, wu_ref[...])
    g4 = gate_ref[s]
    gexp = jnp.concatenate([jnp.broadcast_to(g4[:, e:e + 1], (tm, D_EXPERT)) for e in range(EXPERTS_PER_GROUP)], axis=1)
    acc_ref[...] += _dot((hid * gexp).astype(BF16), wd_ref[...])

    @pl.when(s == N_EXPERT_GROUPS - 1)
    def _():
        y = x1_ref[...] + mod_ref[0, 5:6, :] * acc_ref[...]
        if final:
            ms = jnp.mean(y * y, axis=-1, keepdims=True)
            y = y * lax.rsqrt(ms + EPS) * fg_ref[...]
        out_ref[...] = y


def _ffn_call(oa, ob, oc, od, x, mod, w_out, g_ffn, router_w, router_b, w_gate, w_up, w_down, final_g, final):
    bsz, n, d = x.shape
    tm = TOKEN_TILE
    ntok = bsz * n
    tiles_per_batch = n // tm
    flat = lambda a: a.reshape(ntok, a.shape[-1])
    gw = EXPERTS_PER_GROUP * D_EXPERT
    wg = w_gate.transpose(1, 0, 2).reshape(d, N_EXPERTS * D_EXPERT).astype(BF16)
    wu = w_up.transpose(1, 0, 2).reshape(d, N_EXPERTS * D_EXPERT).astype(BF16)
    wd = w_down.reshape(N_EXPERTS * D_EXPERT, d).astype(BF16)
    tok = lambda width: pl.BlockSpec((tm, width), lambda i, s: (i, 0))
    const = lambda shape: pl.BlockSpec(shape, lambda i, s: (0, 0))
    out = pl.pallas_call(
        functools.partial(_ffn_kernel, final=final),
        grid=(ntok // tm, N_EXPERT_GROUPS),
        in_specs=[tok(GROUP_WIDTH)] * 4 + [
            tok(d), pl.BlockSpec((1, 8, d), lambda i, s: (i // tiles_per_batch, 0, 0)),
            const((d, d)), const((1, d)), const((N_EXPERTS, d)), const((N_EXPERTS, 1)),
            pl.BlockSpec((d, gw), lambda i, s: (0, s)), pl.BlockSpec((d, gw), lambda i, s: (0, s)),
            pl.BlockSpec((gw, d), lambda i, s: (s, 0)), const((1, d))],
        out_specs=pl.BlockSpec((tm, d), lambda i, s: (i, 0)),
        out_shape=jax.ShapeDtypeStruct((ntok, d), F32),
        scratch_shapes=[pltpu.VMEM((tm, d), F32), pltpu.VMEM((tm, d), BF16), pltpu.VMEM((tm, d), F32),
                        pltpu.VMEM((N_EXPERT_GROUPS, tm, LANES), F32)],
        compiler_params=_cparams(("parallel", "arbitrary")),
        name="outproj_moe",
    )(flat(oa), flat(ob), flat(oc), flat(od), flat(x), mod, w_out.astype(BF16), g_ffn.reshape(1, d),
      router_w.T, router_b.reshape(N_EXPERTS, 1), wg, wu, wd, final_g.reshape(1, d))
    return out.reshape(bsz, n, d)


def kernel(x, c, positions, ada_w, ada_b, norm_mix_g, norm_ffn_g, w_in, sgu_norm_g, w_sgu, b_sgu, w_pool,
           pool_scale, w_cmp1, w_cmp2, cmp_pe, w_out, router_w, router_b, w_gate, w_up, w_down, final_norm_g):
    depth = ada_w.shape[0]
    mod = _adaln_mod(c, ada_w, ada_b)
    tables = _rope_call(positions)
    for layer in range(depth):
        (aq, ak, dq, dk, iq, ik4, u, v, cz, dc, vt_all, st) = _inproj_call(
            x, mod[layer], norm_mix_g[layer], w_in[layer], tables)
        k_cmp, v_cmp_t = _cmp_call(dc, w_cmp1[layer], w_cmp2[layer], cmp_pe[layer], positions)
        o_a = _dsa_call(aq, ak, vt_all, iq, ik4, st)
        o_b, o_c = _gmlp_pool_call(u, v, cz, sgu_norm_g[layer], w_sgu[layer], b_sgu[layer], w_pool[layer],
                                   pool_scale[layer])
        o_d = _nsa_call(dq, k_cmp, v_cmp_t, dk, vt_all, st)
        x = _ffn_call(o_a, o_b, o_c, o_d, x, mod[layer], w_out[layer], norm_ffn_g[layer], router_w, router_b,
                      w_gate[layer], w_up[layer], w_down[layer], final_norm_g, final=(layer == depth - 1))
    return x
```

```python
import functools
import math

import numpy as np
import jax
import jax.numpy as jnp
from jax import lax
from jax.experimental import pallas as pl
from jax.experimental.pallas import tpu as pltpu

F32 = jnp.float32
BF16 = jnp.bfloat16
I32 = jnp.int32
I16 = jnp.int16
HIGHEST = lax.Precision.HIGHEST

LANES = 128
HEAD_DIM = 64
GROUP_WIDTH = 256
GROUP_HEADS = 4
ROPE_THETA = 500000.0
EPS = 1e-6
NEG = -1e30
IDX_DIM = 32
DSA_TOPK = 256
CMP_BLOCK = 32
CMP_STRIDE = 16
SEL_BLOCK = 64
SEL_TOPN = 16
WINDOW = 512
N_EXPERTS = 16
EXPERTS_PER_GROUP = 4
N_EXPERT_GROUPS = 4
D_EXPERT = 256
POOL_WINDOWS = (2, 4, 8, 16)

Q_TILE = 128
KEY_CHUNK = 512
ATTN_BLOCK = 512
TOKEN_TILE = 512
VMEM_LIMIT = 56 * 1024 * 1024

INT_MIN = -2 ** 31
QK_SCALE = HEAD_DIM ** -0.5 * math.log2(math.e)
IQ_PARTS = ("hi", "lo", "hi", "zero")
IK_PARTS = ("hi", "hi", "lo", "zero")

_OFF = {}
_o = 0
for _name, _w in (("a_q", 256), ("a_k", 256), ("a_v", 256), ("a_iq", 128), ("a_ik", 32), ("a_iw", 4),
                  ("b_u", 256), ("b_v", 256), ("c_z", 256), ("d_q", 256), ("d_kc", 64), ("d_vc", 64),
                  ("d_ks", 64), ("d_vs", 64), ("d_kw", 64), ("d_vw", 64), ("d_g", 12)):
    _OFF[_name] = (_o, _o + _w)
    _o += _w
ROW_WIDTH = 2432
COL_ROWS = 400


def _cparams(sem):
    return pltpu.CompilerParams(dimension_semantics=sem, vmem_limit_bytes=VMEM_LIMIT)


def _sigmoid(x):
    return 1.0 / (1.0 + jnp.exp(-x))


def _gelu(x):
    return 0.5 * x * (1.0 + jnp.tanh(0.7978845608028654 * (x + 0.044715 * (x * x * x))))


def _split_hi_lo(x, parts):
    hi = x.astype(BF16).astype(F32)
    slot = lax.broadcasted_iota(I32, (1, LANES), 1) // IDX_DIM
    out = jnp.zeros_like(x)
    for j, part in enumerate(parts):
        if part != "zero":
            out = jnp.where(slot == j, hi if part == "hi" else x - hi, out)
    return out.astype(BF16)


def _dot(a, b, **kw):
    return jnp.dot(a, b, preferred_element_type=F32, **kw)


def _dot_nt(a, b, **kw):
    return lax.dot_general(a, b, (((1,), (1,)), ((), ())), preferred_element_type=F32, **kw)


def _mod_kernel(c_ref, w_ref, b_ref, o_ref):
    c = c_ref[...]
    ca = c * _sigmoid(c)
    o_ref[0] = _dot(ca, w_ref[0], precision=HIGHEST) + b_ref[0]


def _adaln_mod(c, ada_w, ada_b):
    depth, d, d6 = ada_w.shape
    bsz = c.shape[0]
    rows = 8
    c_pad = jnp.zeros((rows, d), F32).at[:bsz].set(c)
    tn = 1536
    out = pl.pallas_call(
        _mod_kernel,
        grid=(depth, d6 // tn),
        in_specs=[pl.BlockSpec((rows, d), lambda l, j: (0, 0)),
                  pl.BlockSpec((1, d, tn), lambda l, j: (l, 0, j)),
                  pl.BlockSpec((1, 1, tn), lambda l, j: (l, 0, j))],
        out_specs=pl.BlockSpec((1, rows, tn), lambda l, j: (l, 0, j)),
        out_shape=jax.ShapeDtypeStruct((depth, rows, d6), F32),
        compiler_params=_cparams(("arbitrary", "arbitrary")),
        name="adaln_mod",
    )(c_pad, ada_w, ada_b.reshape(depth, 1, d6))
    mod = out[:, :bsz].reshape(depth, bsz, 6, d)
    return jnp.concatenate([mod, jnp.zeros((depth, bsz, 2, d), F32)], axis=2)


def _freq_row(hd):
    rd = hd // 4
    half = rd // 2
    inv = (ROPE_THETA ** (-np.arange(half, dtype=np.float32) / np.float32(half))).astype(np.float32)
    row = np.zeros((1, LANES), np.float32)
    for lane in range(LANES):
        r = lane % hd
        if r < rd:
            row[0, lane] = inv[r % half]
    return jnp.asarray(row)


def _rope_tables(pos_col, frow, hd):
    half = hd // 8
    ang = pos_col * frow
    c = jnp.cos(ang)
    s = jnp.sin(ang)
    r = lax.broadcasted_iota(I32, (1, LANES), 1) % hd
    sa = jnp.where(r < half, -s, 0.0)
    sb = jnp.where((r >= half) & (r < 2 * half), s, 0.0)
    return c, sa, sb


def _rope_kernel(pos_ref, f64_ref, f32_ref, c64, sa64, sb64, c32, sa32, sb32):
    pos = pos_ref[0].astype(F32)
    c, sa, sb = _rope_tables(pos, f64_ref[...], HEAD_DIM)
    c64[0] = c
    sa64[0] = sa
    sb64[0] = sb
    c, sa, sb = _rope_tables(pos, f32_ref[...], IDX_DIM)
    c32[0] = c
    sa32[0] = sa
    sb32[0] = sb


def _rope_call(positions):
    bsz, n = positions.shape
    tm = TOKEN_TILE
    tab = jax.ShapeDtypeStruct((bsz, n, LANES), F32)
    tspec = pl.BlockSpec((1, tm, LANES), lambda b, i: (b, i, 0))
    fspec = pl.BlockSpec((1, LANES), lambda b, i: (0, 0))
    return pl.pallas_call(
        _rope_kernel,
        grid=(bsz, n // tm),
        in_specs=[pl.BlockSpec((1, tm, 1), lambda b, i: (b, i, 0)), fspec, fspec],
        out_specs=[tspec] * 6,
        out_shape=[tab] * 6,
        compiler_params=_cparams(("parallel", "arbitrary")),
        name="rope_tables",
    )(positions.reshape(bsz, n, 1), _freq_row(HEAD_DIM), _freq_row(IDX_DIM))


def _apply_rope(x, c, sa, sb, half):
    return x * c + pltpu.roll(x, LANES - half, 1) * sa + pltpu.roll(x, half, 1) * sb


def _inproj_kernel(x_ref, mod_ref, g_ref, w_ref, wt_ref, c64, sa64, sb64, c32, sa32, sb32,
                   aq_ref, ak_ref, dq_ref, dk_ref, iq_ref, ik_ref, u_ref, v_ref, cz_ref, dc_ref,
                   vt_ref, st_ref):
    x = x_ref[0]
    ms = jnp.mean(x * x, axis=-1, keepdims=True)
    y = x * lax.rsqrt(ms + EPS) * g_ref[...]
    h = y * (1.0 + mod_ref[0, 1:2, :]) + mod_ref[0, 0:1, :]
    hb = h.astype(BF16)
    z = _dot(hb, w_ref[...])
    zt = _dot_nt(wt_ref[...], hb)

    c, sa, sb = c64[0], sa64[0], sb64[0]
    tiles = [_apply_rope(z[:, LANES * t:LANES * (t + 1)], c, sa, sb, 8) for t in range(7)]
    aq_ref[0] = jnp.concatenate([tiles[0] * QK_SCALE, tiles[1] * QK_SCALE], axis=1).astype(BF16)
    ak_ref[0] = jnp.concatenate([tiles[2], tiles[3]], axis=1).astype(BF16)
    dq_ref[0] = jnp.concatenate([tiles[4] * QK_SCALE, tiles[5] * QK_SCALE], axis=1)
    dk_ref[0] = tiles[6].astype(BF16)
    c, sa, sb = c32[0], sa32[0], sb32[0]
    iq_ref[0] = jnp.concatenate(
        [_split_hi_lo(_apply_rope(z[:, 896 + LANES * h:1024 + LANES * h], c, sa, sb, 4), IQ_PARTS) for h in range(4)],
        axis=1)
    ik_ref[0] = _split_hi_lo(_apply_rope(z[:, 1408:1536], c, sa, sb, 4), IK_PARTS)
    u_ref[0] = z[:, 1536:1792]
    v_ref[0] = z[:, 1792:2048]
    cz_ref[0] = z[:, 2048:2304]
    dc_ref[0] = z[:, 2304:2432]
    vt_ref[0] = zt[0:384].astype(BF16)
    st_ref[0] = zt[384:400]


def _pack_w_in(w_in):
    def col(name):
        a, b = _OFF[name]
        return w_in[:, a:b]
    iq = col("a_iq")
    zero = jnp.zeros((w_in.shape[0], IDX_DIM), w_in.dtype)
    iq_cols = []
    for h in range(4):
        w_h = iq[:, IDX_DIM * h:IDX_DIM * (h + 1)]
        iq_cols += [w_h, w_h, w_h, zero]
    w_row = jnp.concatenate([col("a_q"), col("a_k"), col("d_q"), col("d_ks"), col("d_kw")] + iq_cols +
                            [col("a_ik"), col("a_ik"), col("a_ik"), col("a_ik"),
                             col("b_u"), col("b_v"), col("c_z"), col("d_kc"), col("d_vc")], axis=1)
    w_col = jnp.concatenate([col("a_v"), col("d_vs"), col("d_vw"), col("a_iw"), col("d_g")], axis=1).T
    return w_row.astype(BF16), w_col.astype(BF16)


def _inproj_call(x, mod, g, w_in, tables):
    bsz, n, d = x.shape
    tm = TOKEN_TILE
    w_row, w_col = _pack_w_in(w_in)
    tok = lambda width: pl.BlockSpec((1, tm, width), lambda b, i: (b, i, 0))
    shp = lambda width, dt: jax.ShapeDtypeStruct((bsz, n, width), dt)
    const2 = lambda shape: pl.BlockSpec(shape, lambda b, i: (0, 0))
    return pl.pallas_call(
        _inproj_kernel,
        grid=(bsz, n // tm),
        in_specs=[tok(d), pl.BlockSpec((1, 8, d), lambda b, i: (b, 0, 0)), const2((1, d)),
                  const2((d, ROW_WIDTH)), const2((COL_ROWS, d))] + [tok(LANES)] * 6,
        out_specs=[tok(256), tok(256), tok(256), tok(128), tok(512), tok(128), tok(256), tok(256), tok(256),
                   tok(128),
                   pl.BlockSpec((1, 384, tm), lambda b, i: (b, 0, i)),
                   pl.BlockSpec((1, 16, tm), lambda b, i: (b, 0, i))],
        out_shape=[shp(256, BF16), shp(256, BF16), shp(256, F32), shp(128, BF16), shp(512, BF16), shp(128, BF16),
                   shp(256, F32), shp(256, F32), shp(256, F32), shp(128, F32),
                   jax.ShapeDtypeStruct((bsz, 384, n), BF16),
                   jax.ShapeDtypeStruct((bsz, 16, n), F32)],
        compiler_params=_cparams(("parallel", "arbitrary")),
        name="inproj",
    )(x, mod, g.reshape(1, d), w_row, w_col, *tables)


def _cmp_kernel(kf_ref, vf_ref, w1a_ref, w1b_ref, w1_ref, pe_ref, w2k_ref, w2vt_ref, pos_ref, f_ref,
                kc_ref, vct_ref):
    nc = kf_ref.shape[1]

    def hidden(flat, idx):
        fb = flat.astype(BF16)
        a = _dot(fb, w1a_ref[idx])
        b = _dot(fb, w1b_ref[idx])
        b_next = pltpu.roll(b, nc - 1, 0)
        bias = _dot(jnp.broadcast_to(pe_ref[idx], (8, pe_ref.shape[2])), w1_ref[idx], precision=HIGHEST)[0:1]
        return _gelu(a + b_next + bias).astype(BF16)

    k_c = _dot(hidden(kf_ref[0], 0), w2k_ref[...])
    c, sa, sb = _rope_tables(pos_ref[0].astype(F32), f_ref[...], HEAD_DIM)
    kc_ref[0] = _apply_rope(k_c, c, sa, sb, 8).astype(BF16)
    vct_ref[0] = _dot_nt(w2vt_ref[...], hidden(vf_ref[0], 1)).astype(BF16)


def _cmp_call(dc, w_cmp1, w_cmp2, cmp_pe, positions):
    bsz, n, _ = dc.shape
    nc = n // CMP_STRIDE
    half = CMP_STRIDE * HEAD_DIM
    flat = dc.reshape(bsz, nc, CMP_STRIDE, 2, HEAD_DIM).transpose(3, 0, 1, 2, 4).reshape(2, bsz, nc, half)
    w1 = w_cmp1.reshape(2, CMP_BLOCK * HEAD_DIM, HEAD_DIM)
    w1a = w1[:, :half].astype(BF16)
    w1b = w1[:, half:].astype(BF16)
    pe = cmp_pe.reshape(2, 1, CMP_BLOCK * HEAD_DIM)
    w2k = jnp.concatenate([w_cmp2[0], jnp.zeros_like(w_cmp2[0])], axis=1).astype(BF16)
    w2vt = w_cmp2[1].T.astype(BF16)
    mid = jnp.minimum(jnp.arange(nc) * CMP_STRIDE + CMP_BLOCK // 2, n - 1)
    pos_c = positions[:, mid].reshape(bsz, nc, 1)
    per_b = lambda shape: pl.BlockSpec(shape, lambda b: (b, 0, 0))
    const = lambda shape: pl.BlockSpec(shape, lambda b: tuple(0 for _ in shape))
    return pl.pallas_call(
        _cmp_kernel,
        grid=(bsz,),
        in_specs=[per_b((1, nc, half)), per_b((1, nc, half)), const((2, half, HEAD_DIM)), const((2, half, HEAD_DIM)),
                  const((2, 2 * half, HEAD_DIM)), const((2, 1, 2 * half)), const((HEAD_DIM, LANES)),
                  const((HEAD_DIM, HEAD_DIM)), per_b((1, nc, 1)), const((1, LANES))],
        out_specs=[per_b((1, nc, LANES)), per_b((1, HEAD_DIM, nc))],
        out_shape=[jax.ShapeDtypeStruct((bsz, nc, LANES), BF16), jax.ShapeDtypeStruct((bsz, HEAD_DIM, nc), BF16)],
        compiler_params=_cparams(("arbitrary",)),
        name="nsa_compress",
    )(flat[0], flat[1], w1a, w1b, w1, pe, w2k, w2vt, pos_c, _freq_row(HEAD_DIM))


def _head_masked(q, h, width):
    lane = lax.broadcasted_iota(I32, (1, width), 1)
    return jnp.where((lane // HEAD_DIM) == h, q, jnp.zeros_like(q))


def _online_step(state, s, bias, vt):
    m, l, acc = state
    s = s + bias
    m_new = jnp.maximum(m, jnp.max(s, axis=0, keepdims=True))
    alpha = jnp.exp2(m - m_new)
    p = jnp.exp2(s - m_new)
    l = alpha * l + jnp.sum(p, axis=0, keepdims=True)
    acc = alpha * acc + _dot(vt, p.astype(BF16))
    return m_new, l, acc


def _init_state():
    return (jnp.full((1, Q_TILE), NEG, F32), jnp.zeros((1, Q_TILE), F32), jnp.zeros((HEAD_DIM, Q_TILE), F32))


def _float_key(x):
    bits = pltpu.bitcast(x + 0.0, I32)
    return bits ^ ((bits >> 31) & jnp.int32(0x7FFFFFFF))


def _radix_kth(count_ge, k):
    def bit_body(bi, thr):
        cand = thr ^ (jnp.full((1, Q_TILE), 1, I32) << (31 - bi))
        return jnp.where(count_ge(cand) >= k, cand, thr)
    return lax.fori_loop(0, 32, bit_body, jnp.full((1, Q_TILE), INT_MIN, I32))


def _radix_kth16(count_ge, k):
    def bit_body(bi, thr):
        cand = thr + (jnp.full((1, Q_TILE), 1, I32) << (15 - bi))
        return jnp.where(count_ge(cand.astype(I16)) >= k, cand, thr)
    return lax.fori_loop(0, 16, bit_body, jnp.full((1, Q_TILE), -32768, I32))


def _dsa_kernel(q_ref, k_ref, vt_ref, iq_ref, ik_ref, st_ref, tri_ref, o_ref, keys_ref, hi_ref, lo_ref):
    i = pl.program_id(1)
    ck = KEY_CHUNK
    ab = ATTN_BLOCK
    nck = (i * Q_TILE + Q_TILE + ck - 1) // ck
    nblk = (i * Q_TILE + Q_TILE + ab - 1) // ab
    t_row = i * Q_TILE + lax.broadcasted_iota(I32, (1, Q_TILE), 1)

    iqp = iq_ref[0]
    iq_stack = jnp.concatenate([iqp[:, LANES * h:LANES * (h + 1)] for h in range(4)], axis=0)
    iw = [st_ref[0, h:h + 1, :] for h in range(4)]

    sub = Q_TILE
    sub_col = lax.broadcasted_iota(I32, (sub, 1), 0)

    def score_body(c, _):
        for r in range(ck // sub):
            off = pl.multiple_of(c * ck + r * sub, sub)
            s_all = _dot_nt(ik_ref[0, pl.ds(off, sub), :], iq_stack)
            sc = iw[0] * jnp.maximum(s_all[:, 0:LANES], 0.0)
            for h in range(1, 4):
                sc = sc + iw[h] * jnp.maximum(s_all[:, LANES * h:LANES * (h + 1)], 0.0)
            keys = jnp.where(off + sub_col <= t_row, _float_key(sc), INT_MIN)
            keys_ref[pl.ds(off, sub), :] = keys
            hi_ref[pl.ds(off, sub), :] = (keys >> 16).astype(I16)
        return 0

    lax.fori_loop(0, nck, score_body, 0)

    def count(pred):
        def body(c, acc):
            kc = keys_ref[pl.ds(pl.multiple_of(c * ck, ck), ck), :]
            return acc + jnp.sum(jnp.where(pred(kc), 1, 0).astype(I32).reshape(ck // 8, 8, Q_TILE), axis=0)
        acc = lax.fori_loop(0, nck, body, jnp.zeros((8, Q_TILE), I32))
        return jnp.sum(acc, axis=0, keepdims=True)

    def count16(ref, pred):
        def body(c, acc):
            w = jnp.where(pred(ref[pl.ds(pl.multiple_of(c * ck, ck), ck), :]), jnp.int16(1), jnp.int16(0))
            parts = [w[16 * j:16 * (j + 1)] for j in range(ck // 16)]
            while len(parts) > 1:
                parts = [parts[2 * j] + parts[2 * j + 1] for j in range(len(parts) // 2)]
            return acc + parts[0]
        acc = lax.fori_loop(0, nck, body, jnp.zeros((16, Q_TILE), I16))
        return jnp.sum(acc.astype(I32), axis=0, keepdims=True)

    hi_thr = _radix_kth16(lambda c16: count16(hi_ref, lambda v: v >= c16), DSA_TOPK)
    hi16 = hi_thr.astype(I16)
    k_low = DSA_TOPK - count16(hi_ref, lambda v: v > hi16)

    def low_body(c, _):
        rows = pl.ds(pl.multiple_of(c * ck, ck), ck)
        kc = keys_ref[rows, :]
        lo_ref[rows, :] = jnp.where((kc >> 16) == hi_thr, (kc & 0xFFFF) - 32768, -32768).astype(I16)
        return 0

    lax.fori_loop(0, nck, low_body, 0)
    lo_thr = _radix_kth16(lambda c16: count16(lo_ref, lambda v: v >= c16), k_low)
    thr = (hi_thr << 16) + (lo_thr + 32768)
    n_ge = count(lambda kc: kc >= thr)
    need = (DSA_TOPK - count(lambda kc: kc > thr)).astype(F32)
    excess_ties = jnp.max(jnp.where((n_ge > DSA_TOPK) & (thr > INT_MIN), 1, 0)) > 0
    thr_floor = jnp.maximum(thr, INT_MIN + 1)

    q = q_ref[0]
    q_stack = jnp.concatenate([_head_masked(q, h, GROUP_WIDTH) for h in range(4)], axis=0)
    blk_col = lax.broadcasted_iota(I32, (ab, 1), 0)

    def attend(rank_ties):
        def body(c, carry):
            states, tie_carry = carry
            off = pl.multiple_of(c * ab, ab)
            kc = keys_ref[pl.ds(off, ab), :]
            if rank_ties:
                eq = kc == thr
                eq_f = jnp.where(eq, 1.0, 0.0)
                rank = _dot(tri_ref[0:ab, 0:ab], eq_f.astype(BF16)) + tie_carry
                sel = ((kc > thr) | (eq & (rank < need))) & (off + blk_col <= t_row)
                tie_carry = tie_carry + jnp.sum(eq_f, axis=0, keepdims=True)
            else:
                sel = kc >= thr_floor
            bias = jnp.where(sel, 0.0, NEG)
            s_all = _dot_nt(k_ref[0, pl.ds(off, ab), :], q_stack)
            states = tuple(
                _online_step(states[h], s_all[:, LANES * h:LANES * (h + 1)], bias,
                             vt_ref[0, h * HEAD_DIM:(h + 1) * HEAD_DIM, pl.ds(off, ab)]) for h in range(4))
            return states, tie_carry

        init = (tuple(_init_state() for _ in range(4)), jnp.zeros((1, Q_TILE), F32))
        return lax.fori_loop(0, nblk, body, init)[0]

    states = lax.cond(excess_ties, lambda: attend(True), lambda: attend(False))
    out_t = jnp.concatenate([acc / l for (_, l, acc) in states], axis=0)
    o_ref[0] = out_t.T.astype(BF16)


def _strict_lower(size):
    return jnp.asarray(np.tril(np.ones((size, size), np.float32), -1)).astype(BF16)


def _dsa_call(aq, ak, vt_all, iqp, ikp, st):
    bsz, n, _ = aq.shape
    return pl.pallas_call(
        _dsa_kernel,
        grid=(bsz, n // Q_TILE),
        in_specs=[pl.BlockSpec((1, Q_TILE, GROUP_WIDTH), lambda b, i: (b, i, 0)),
                  pl.BlockSpec((1, n, GROUP_WIDTH), lambda b, i: (b, 0, 0)),
                  pl.BlockSpec((1, GROUP_WIDTH, n), lambda b, i: (b, 0, 0)),
                  pl.BlockSpec((1, Q_TILE, 4 * LANES), lambda b, i: (b, i, 0)),
                  pl.BlockSpec((1, n, LANES), lambda b, i: (b, 0, 0)),
                  pl.BlockSpec((1, 16, Q_TILE), lambda b, i: (b, 0, i)),
                  pl.BlockSpec((ATTN_BLOCK, ATTN_BLOCK), lambda b, i: (0, 0))],
        out_specs=pl.BlockSpec((1, Q_TILE, GROUP_WIDTH), lambda b, i: (b, i, 0)),
        out_shape=jax.ShapeDtypeStruct((bsz, n, GROUP_WIDTH), BF16),
        scratch_shapes=[pltpu.VMEM((n, Q_TILE), I32), pltpu.VMEM((n, Q_TILE), I16), pltpu.VMEM((n, Q_TILE), I16)],
        compiler_params=_cparams(("parallel", "arbitrary")),
        name="dsa_attention",
    )(aq, ak, vt_all, iqp, ikp, st, _strict_lower(ATTN_BLOCK))


def _nsa_kernel(q_ref, kc_ref, vct_ref, dk_ref, vst_ref, vwt_ref, st_ref, ov_ref, tri_ref, o_ref, sel_ref):
    i = pl.program_id(1)
    nc = kc_ref.shape[1]
    nsel = ov_ref.shape[0]
    ab = ATTN_BLOCK
    nblk = (i * Q_TILE + Q_TILE + ab - 1) // ab
    t_row = i * Q_TILE + lax.broadcasted_iota(I32, (1, Q_TILE), 1)
    lane = lax.broadcasted_iota(I32, (1, LANES), 1)

    q = q_ref[0]
    q_sel, q_win = [], []
    for h in range(4):
        tile = q[:, LANES * (h // 2):LANES * (h // 2 + 1)]
        swapped = pltpu.roll(tile, HEAD_DIM, 1)
        low, high = (tile, swapped) if h % 2 == 0 else (swapped, tile)
        q_sel.append(jnp.where(lane < HEAD_DIM, low, 0.0).astype(BF16))
        q_win.append(jnp.where(lane >= HEAD_DIM, high, 0.0).astype(BF16))

    qs_stack = jnp.concatenate(q_sel, axis=0)
    qw_stack = jnp.concatenate(q_win, axis=0)

    n_col = lax.broadcasted_iota(I32, (nc, 1), 0)
    vis_bias = jnp.where(n_col * CMP_STRIDE + (CMP_BLOCK - 1) <= t_row, 0.0, NEG)
    any_vis = t_row >= CMP_BLOCK - 1
    s_all = _dot_nt(kc_ref[0], qs_stack)
    o_cmp = []
    p_sum = jnp.zeros((nc, Q_TILE), F32)
    for h in range(4):
        s = s_all[:, LANES * h:LANES * (h + 1)] + vis_bias
        p = jnp.exp2(s - jnp.max(s, axis=0, keepdims=True))
        l = jnp.sum(p, axis=0, keepdims=True)
        p = p * jnp.where(any_vis, 1.0 / l, 0.0)
        p_sum = p_sum + p
        o_cmp.append(_dot(vct_ref[0], p.astype(BF16)))

    p_hi = p_sum.astype(BF16)
    p_lo = (p_sum - p_hi.astype(F32)).astype(BF16)
    imp = _dot(ov_ref[...], p_hi) + _dot(ov_ref[...], p_lo)
    j_col = lax.broadcasted_iota(I32, (nsel, 1), 0)
    forced = (j_col == t_row // SEL_BLOCK) | (j_col == 0)
    admissible = j_col * SEL_BLOCK <= t_row
    keys = jnp.where(admissible, _float_key(jnp.where(forced, jnp.inf, imp)), INT_MIN)

    def count(mask):
        return jnp.sum(jnp.sum(jnp.where(mask, 1, 0).astype(I32).reshape(nsel // 8, 8, Q_TILE), axis=0),
                       axis=0, keepdims=True)

    top_n = min(SEL_TOPN, nsel)
    thr = _radix_kth(lambda cand: count(keys >= cand), top_n)
    need = (top_n - count(keys > thr)).astype(F32)
    eq = keys == thr
    rank = _dot(tri_ref[...], jnp.where(eq, 1.0, 0.0).astype(BF16))
    chosen = ((keys > thr) | (eq & (rank < need))) & admissible
    sel_bias = jnp.where(chosen, 0.0, NEG)
    for j in range(nsel):
        sel_ref[j] = jnp.broadcast_to(sel_bias[j:j + 1, :], (8, Q_TILE))

    blk_col = lax.broadcasted_iota(I32, (ab, 1), 0)
    per_blk = ab // SEL_BLOCK

    def sel_body(c, states):
        off = pl.multiple_of(c * ab, ab)
        bias = jnp.concatenate([jnp.tile(sel_ref[c * per_blk + r], (SEL_BLOCK // 8, 1)) for r in range(per_blk)], axis=0)
        bias = jnp.where(off + blk_col <= t_row, bias, NEG)
        s_blk = _dot_nt(dk_ref[0, pl.ds(off, ab), :], qs_stack)
        vt = vst_ref[0, :, pl.ds(off, ab)]
        return tuple(_online_step(states[h], s_blk[:, LANES * h:LANES * (h + 1)], bias, vt) for h in range(4))

    st_sel = lax.fori_loop(0, nblk, sel_body, tuple(_init_state() for _ in range(4)))

    span = WINDOW + Q_TILE
    start = pl.multiple_of(jnp.maximum(i * Q_TILE - WINDOW, 0), Q_TILE)
    kpos = start + lax.broadcasted_iota(I32, (span, 1), 0)
    win_bias = jnp.where((kpos <= t_row) & (kpos > t_row - WINDOW), 0.0, NEG)
    s_win = _dot_nt(dk_ref[0, pl.ds(start, span), :], qw_stack)
    vw = vwt_ref[0, :, pl.ds(start, span)]
    st_win = [_online_step(_init_state(), s_win[:, LANES * h:LANES * (h + 1)], win_bias, vw) for h in range(4)]

    outs = []
    for h in range(4):
        g = [_sigmoid(st_ref[0, 4 + 3 * h + r:5 + 3 * h + r, :]) for r in range(3)]
        o_s = st_sel[h][2] / st_sel[h][1]
        o_w = st_win[h][2] / st_win[h][1]
        outs.append(g[0] * o_cmp[h] + g[1] * o_s + g[2] * o_w)
    o_ref[0] = jnp.concatenate(outs, axis=0).T.astype(BF16)


def _overlap_matrix(n):
    nc = n // CMP_STRIDE
    nsel = n // SEL_BLOCK
    sel_start = np.arange(nsel) * SEL_BLOCK
    cmp_start = np.arange(nc) * CMP_STRIDE
    ov = np.clip(np.minimum(sel_start[:, None] + SEL_BLOCK, cmp_start[None, :] + CMP_BLOCK)
                 - np.maximum(sel_start[:, None], cmp_start[None, :]), 0, None).astype(np.float32) / CMP_BLOCK
    ov[:, nc - 1] = 0.0
    return jnp.asarray(ov).astype(BF16)


def _nsa_call(dq, k_cmp, v_cmp_t, dk, vt_all, st):
    bsz, n, _ = dq.shape
    nc = n // CMP_STRIDE
    nsel = n // SEL_BLOCK
    return pl.pallas_call(
        _nsa_kernel,
        grid=(bsz, n // Q_TILE),
        in_specs=[pl.BlockSpec((1, Q_TILE, GROUP_WIDTH), lambda b, i: (b, i, 0)),
                  pl.BlockSpec((1, nc, LANES), lambda b, i: (b, 0, 0)),
                  pl.BlockSpec((1, HEAD_DIM, nc), lambda b, i: (b, 0, 0)),
                  pl.BlockSpec((1, n, LANES), lambda b, i: (b, 0, 0)),
                  pl.BlockSpec((1, HEAD_DIM, n), lambda b, i: (b, 4, 0)),
                  pl.BlockSpec((1, HEAD_DIM, n), lambda b, i: (b, 5, 0)),
                  pl.BlockSpec((1, 16, Q_TILE), lambda b, i: (b, 0, i)),
                  pl.BlockSpec((nsel, nc), lambda b, i: (0, 0)),
                  pl.BlockSpec((nsel, nsel), lambda b, i: (0, 0))],
        out_specs=pl.BlockSpec((1, Q_TILE, GROUP_WIDTH), lambda b, i: (b, i, 0)),
        out_shape=jax.ShapeDtypeStruct((bsz, n, GROUP_WIDTH), BF16),
        scratch_shapes=[pltpu.VMEM((nsel, 8, Q_TILE), F32)],
        compiler_params=_cparams(("parallel", "arbitrary")),
        name="nsa_attention",
    )(dq, k_cmp, v_cmp_t, dk, vt_all, vt_all, st, _overlap_matrix(n), _strict_lower(nsel))


def _gmlp_pool_kernel(u_ref, v_ref, z_ref, zp_ref, ng_ref, ws_ref, bs_ref, wp_ref, ps_ref, ob_ref, oc_ref):
    i = pl.program_id(1)
    chunk = u_ref.shape[1]
    lane = lax.broadcasted_iota(I32, (1, GROUP_WIDTH), 1)

    u = _gelu(u_ref[0])
    v = _gelu(v_ref[0])
    mu = jnp.mean(v, axis=-1, keepdims=True)
    vc = v - mu
    var = jnp.mean(vc * vc, axis=-1, keepdims=True)
    vn = (vc * lax.rsqrt(var + EPS) * ng_ref[...]).astype(BF16)
    row = lax.broadcasted_iota(I32, (chunk, chunk), 0)
    colm = lax.broadcasted_iota(I32, (chunk, chunk), 1)
    mixed = bs_ref[...]
    for h in range(GROUP_HEADS):
        w = jnp.where(colm <= row, ws_ref[h], 0.0).astype(BF16)
        mixed = mixed + jnp.where((lane // HEAD_DIM) == h, _dot(w, vn), 0.0)
    ob_ref[0] = (u * mixed).astype(BF16)

    z = z_ref[0]
    prev = jnp.where(i > 0, zp_ref[0, chunk - 16:chunk, :], 0.0)
    rows = chunk + 16
    acc = jnp.concatenate([prev, z], axis=0)
    t_col = i * chunk + lax.broadcasted_iota(I32, (chunk, 1), 0)
    pooled = jnp.zeros((chunk, GROUP_WIDTH), F32)
    pool_ch = GROUP_WIDTH // len(POOL_WINDOWS)
    shift = 1
    for g, w in enumerate(POOL_WINDOWS):
        acc = acc + pltpu.roll(acc, shift, 0)
        shift *= 2
        count = jnp.minimum(t_col + 1, w).astype(F32)
        pooled = jnp.where((lane // pool_ch) == g, acc[16:rows] / count, pooled)
    pooled = pooled - z
    oc_ref[0] = (_dot(pooled.astype(BF16), wp_ref[...]) * ps_ref[...]).astype(BF16)


def _gmlp_pool_call(u, v, cz, sgu_g, w_sgu, b_sgu, w_pool, pool_scale):
    bsz, n, _ = u.shape
    chunk = Q_TILE
    bias = jnp.repeat(b_sgu.T, HEAD_DIM, axis=1)
    ngrp = len(POOL_WINDOWS)
    pc = GROUP_WIDTH // ngrp
    wbd = jnp.zeros((GROUP_WIDTH, GROUP_WIDTH), F32)
    for g in range(ngrp):
        wbd = wbd.at[g * pc:(g + 1) * pc, g * pc:(g + 1) * pc].set(w_pool[g])
    tok = pl.BlockSpec((1, chunk, GROUP_WIDTH), lambda b, i: (b, i, 0))
    const = lambda shape: pl.BlockSpec(shape, lambda b, i: tuple(0 for _ in shape))
    out = jax.ShapeDtypeStruct((bsz, n, GROUP_WIDTH), BF16)
    return pl.pallas_call(
        _gmlp_pool_kernel,
        grid=(bsz, n // chunk),
        in_specs=[tok, tok, tok,
                  pl.BlockSpec((1, chunk, GROUP_WIDTH), lambda b, i: (b, jnp.maximum(i - 1, 0), 0)),
                  const((1, GROUP_WIDTH)), const((GROUP_HEADS, chunk, chunk)), const((chunk, GROUP_WIDTH)),
                  const((GROUP_WIDTH, GROUP_WIDTH)), const((1, GROUP_WIDTH))],
        out_specs=[tok, tok],
        out_shape=[out, out],
        compiler_params=_cparams(("parallel", "arbitrary")),
        name="gmlp_pool",
    )(u, v, cz, cz, sgu_g.reshape(1, -1), w_sgu, bias, wbd.astype(BF16), pool_scale.reshape(1, -1))


def _ffn_kernel(oa_ref, ob_ref, oc_ref, od_ref, x_ref, mod_ref, wo_ref, g_ref, rwt_ref, rb_ref,
                wg_ref, wu_ref, wd_ref, fg_ref, out_ref, x1_ref, h_ref, acc_ref, gate_ref, *, final):
    s = pl.program_id(1)
    tm = x_ref.shape[0]

    @pl.when(s == 0)
    def _():
        mixed = _dot(oa_ref[...], wo_ref[0:256, :])
        mixed = mixed + _dot(ob_ref[...], wo_ref[256:512, :])
        mixed = mixed + _dot(oc_ref[...], wo_ref[512:768, :])
        mixed = mixed + _dot(od_ref[...], wo_ref[768:1024, :])
        x1 = x_ref[...] + mod_ref[0, 2:3, :] * mixed
        x1_ref[...] = x1
        ms = jnp.mean(x1 * x1, axis=-1, keepdims=True)
        h = x1 * lax.rsqrt(ms + EPS) * g_ref[...] * (1.0 + mod_ref[0, 4:5, :]) + mod_ref[0, 3:4, :]
        h_ref[...] = h.astype(BF16)
        acc_ref[...] = jnp.zeros_like(acc_ref)

        aff = _sigmoid(_dot_nt(rwt_ref[...], h, precision=HIGHEST))
        biased = aff + rb_ref[...]
        r = lax.broadcasted_iota(I32, (N_EXPERTS, 1), 0)
        best_score = None
        best_group = None
        for g in range(N_EXPERT_GROUPS):
            b = [biased[EXPERTS_PER_GROUP * g + e:EXPERTS_PER_GROUP * g + e + 1, :] for e in range(EXPERTS_PER_GROUP)]
            top1 = jnp.maximum(jnp.maximum(b[0], b[1]), jnp.maximum(b[2], b[3]))
            pair_mins = [jnp.minimum(b[a], b[c]) for a in range(4) for c in range(a + 1, 4)]
            top2 = functools.reduce(jnp.maximum, pair_mins)
            score = top1 + top2
            if g == 0:
                best_score, best_group = score, jnp.zeros_like(score, dtype=I32)
            else:
                better = score > best_score
                best_score = jnp.where(better, score, best_score)
                best_group = jnp.where(better, g, best_group)
        cand = jnp.where((r // EXPERTS_PER_GROUP) == best_group, biased, -jnp.inf)
        m1 = jnp.max(cand, axis=0, keepdims=True)
        i1 = jnp.min(jnp.where(cand == m1, r, N_EXPERTS), axis=0, keepdims=True)
        cand2 = jnp.where(r == i1, -jnp.inf, cand)
        m2 = jnp.max(cand2, axis=0, keepdims=True)
        i2 = jnp.min(jnp.where(cand2 == m2, r, N_EXPERTS), axis=0, keepdims=True)
        a1 = jnp.sum(jnp.where(r == i1, aff, 0.0), axis=0, keepdims=True)
        a2 = jnp.sum(jnp.where(r == i2, aff, 0.0), axis=0, keepdims=True)
        tot = a1 + a2
        gate_t = jnp.where(r == i1, a1 / tot, 0.0) + jnp.where(r == i2, a2 / tot, 0.0)
        gate_tok = jnp.concatenate([gate_t, jnp.zeros((LANES - N_EXPERTS, tm), F32)], axis=0).T
        gate_ref[0] = gate_tok
        for c in range(1, N_EXPERT_GROUPS):
            gate_ref[c] = pltpu.roll(gate_tok, LANES - EXPERTS_PER_GROUP * c, 1)

    hb = h_ref[...]
    hid = _dot(hb, wg_ref[...])
    hid = hid * _sigmoid(hid) * _dot(hb, wu_ref[...])
    g4 = gate_ref[s]
    gexp = jnp.concatenate([jnp.broadcast_to(g4[:, e:e + 1], (tm, D_EXPERT)) for e in range(EXPERTS_PER_GROUP)], axis=1)
    acc_ref[...] += _dot((hid * gexp).astype(BF16), wd_ref[...])

    @pl.when(s == N_EXPERT_GROUPS - 1)
    def _():
        y = x1_ref[...] + mod_ref[0, 5:6, :] * acc_ref[...]
        if final:
            ms = jnp.mean(y * y, axis=-1, keepdims=True)
            y = y * lax.rsqrt(ms + EPS) * fg_ref[...]
        out_ref[...] = y


def _ffn_call(oa, ob, oc, od, x, mod, w_out, g_ffn, router_w, router_b, w_gate, w_up, w_down, final_g, final):
    bsz, n, d = x.shape
    tm = TOKEN_TILE
    ntok = bsz * n
    tiles_per_batch = n // tm
    flat = lambda a: a.reshape(ntok, a.shape[-1])
    gw = EXPERTS_PER_GROUP * D_EXPERT
    wg = w_gate.transpose(1, 0, 2).reshape(d, N_EXPERTS * D_EXPERT).astype(BF16)
    wu = w_up.transpose(1, 0, 2).reshape(d, N_EXPERTS * D_EXPERT).astype(BF16)
    wd = w_down.reshape(N_EXPERTS * D_EXPERT, d).astype(BF16)
    tok = lambda width: pl.BlockSpec((tm, width), lambda i, s: (i, 0))
    const = lambda shape: pl.BlockSpec(shape, lambda i, s: (0, 0))
    out = pl.pallas_call(
        functools.partial(_ffn_kernel, final=final),
        grid=(ntok // tm, N_EXPERT_GROUPS),
        in_specs=[tok(GROUP_WIDTH)] * 4 + [
            tok(d), pl.BlockSpec((1, 8, d), lambda i, s: (i // tiles_per_batch, 0, 0)),
            const((d, d)), const((1, d)), const((N_EXPERTS, d)), const((N_EXPERTS, 1)),
            pl.BlockSpec((d, gw), lambda i, s: (0, s)), pl.BlockSpec((d, gw), lambda i, s: (0, s)),
            pl.BlockSpec((gw, d), lambda i, s: (s, 0)), const((1, d))],
        out_specs=pl.BlockSpec((tm, d), lambda i, s: (i, 0)),
        out_shape=jax.ShapeDtypeStruct((ntok, d), F32),
        scratch_shapes=[pltpu.VMEM((tm, d), F32), pltpu.VMEM((tm, d), BF16), pltpu.VMEM((tm, d), F32),
                        pltpu.VMEM((N_EXPERT_GROUPS, tm, LANES), F32)],
        compiler_params=_cparams(("parallel", "arbitrary")),
        name="outproj_moe",
    )(flat(oa), flat(ob), flat(oc), flat(od), flat(x), mod, w_out.astype(BF16), g_ffn.reshape(1, d),
      router_w.T, router_b.reshape(N_EXPERTS, 1), wg, wu, wd, final_g.reshape(1, d))
    return out.reshape(bsz, n, d)


def kernel(x, c, positions, ada_w, ada_b, norm_mix_g, norm_ffn_g, w_in, sgu_norm_g, w_sgu, b_sgu, w_pool,
           pool_scale, w_cmp1, w_cmp2, cmp_pe, w_out, router_w, router_b, w_gate, w_up, w_down, final_norm_g):
    depth = ada_w.shape[0]
    mod = _adaln_mod(c, ada_w, ada_b)
    tables = _rope_call(positions)
    for layer in range(depth):
        (aq, ak, dq, dk, iq, ik4, u, v, cz, dc, vt_all, st) = _inproj_call(
            x, mod[layer], norm_mix_g[layer], w_in[layer], tables)
        k_cmp, v_cmp_t = _cmp_call(dc, w_cmp1[layer], w_cmp2[layer], cmp_pe[layer], positions)
        o_a = _dsa_call(aq, ak, vt_all, iq, ik4, st)
        o_b, o_c = _gmlp_pool_call(u, v, cz, sgu_norm_g[layer], w_sgu[layer], b_sgu[layer], w_pool[layer],
                                   pool_scale[layer])
        o_d = _nsa_call(dq, k_cmp, v_cmp_t, dk, vt_all, st)
        x = _ffn_call(o_a, o_b, o_c, o_d, x, mod[layer], w_out[layer], norm_ffn_g[layer], router_w, router_b,
                      w_gate[layer], w_up[layer], w_down[layer], final_norm_g, final=(layer == depth - 1))
    return x
```
